```python
import jax, jax.numpy as jnp
from jax import lax
import numpy as np

D_MODEL = 1024
BATCH = 8
SEQ = 4096
DEPTH = 2

GRID_W = 64
WIN_H_MAX = 8
WIN_W = 16
COL_BLOCK = 16
COL_SLAB = 32
NA_HEADS = 8
NA_HEAD_DIM = 64
NA_WIDTH = NA_HEADS * NA_HEAD_DIM
GLA_HEADS = 4
GLA_DK = 64
GLA_DV = 128
GLA_QK_WIDTH = GLA_HEADS * GLA_DK
GLA_V_WIDTH = GLA_HEADS * GLA_DV
GLA_GATE_RANK = 16
GLA_TAU = 16.0
GLA_CHUNK = 64
MIX_WIDTH = NA_WIDTH + GLA_V_WIDTH
IN_SPLITS = (NA_WIDTH, NA_WIDTH, NA_WIDTH,
             GLA_QK_WIDTH, GLA_QK_WIDTH,
             GLA_V_WIDTH, GLA_V_WIDTH,
             GLA_GATE_RANK, GLA_GATE_RANK)
IN_WIDTH = sum(IN_SPLITS)
D_FF = 2816
EPS = 1e-6
NEG_INF = -1e30

kernel_name = "hybrid_na_gla_macaron_encoder"


def rmsnorm(x, g):
    xf = x.astype(jnp.float32)
    y = xf * lax.rsqrt(jnp.mean(xf * xf, axis=-1, keepdims=True) + EPS)
    return (y * g.astype(jnp.float32)).astype(x.dtype)


def swiglu(h, w_gate, w_up, w_down):
    return (jax.nn.silu(h @ w_gate) * (h @ w_up)) @ w_down


def split_heads(a, n):
    b, s, w = a.shape
    return a.reshape(b, s, n, w // n).transpose(0, 2, 1, 3)


def merge_heads(a):
    b, n, s, d = a.shape
    return a.transpose(0, 2, 1, 3).reshape(b, s, n * d)


def neighbourhood_attention(q, k, v, rpb):
    B, H, S, d = q.shape
    rows = S // GRID_W
    wh = min(WIN_H_MAX, rows)
    q = q.reshape(B, H, rows, GRID_W, d)
    k = k.reshape(B, H, rows, GRID_W, d)
    v = v.reshape(B, H, rows, GRID_W, d)
    r = np.arange(rows)
    row_start = np.clip(r - wh // 2, 0, rows - wh)
    key_rows = row_start[:, None] + np.arange(wh)[None, :]
    dr = key_rows - r[:, None] + (WIN_H_MAX - 1)
    scale = NA_HEAD_DIM ** -0.5
    outs = []
    for j in range(GRID_W // COL_BLOCK):
        qc = np.arange(j * COL_BLOCK, (j + 1) * COL_BLOCK)
        col_start = np.clip(qc - WIN_W // 2, 0, GRID_W - WIN_W)
        c0 = int(np.clip(j * COL_BLOCK - WIN_W // 2, 0, GRID_W - COL_SLAB))
        kc = c0 + np.arange(COL_SLAB)
        valid = (kc[None, :] >= col_start[:, None]) & (kc[None, :] < col_start[:, None] + WIN_W)
        dc = np.clip(kc[None, :] - qc[:, None] + (WIN_W - 1), 0, 2 * WIN_W - 2)
        qb = q[:, :, :, j * COL_BLOCK:(j + 1) * COL_BLOCK]
        kb = k[:, :, :, c0:c0 + COL_SLAB][:, :, key_rows]
        vb = v[:, :, :, c0:c0 + COL_SLAB][:, :, key_rows]
        s = jnp.einsum('bhrqd,bhrwkd->bhrqwk', qb, kb).astype(jnp.float32) * scale
        bias = rpb[:, dr[:, None, :, None], dc[None, :, None, :]]
        s = s + bias[None].astype(jnp.float32)
        s = jnp.where(valid[:, None, :], s, NEG_INF)
        p = jax.nn.softmax(s.reshape(B, H, rows, COL_BLOCK, wh * COL_SLAB), axis=-1)
        p = p.reshape(B, H, rows, COL_BLOCK, wh, COL_SLAB).astype(v.dtype)
        outs.append(jnp.einsum('bhrqwk,bhrwkd->bhrqd', p, vb))
    o = jnp.concatenate(outs, axis=3)
    return o.reshape(B, H, S, d)


def gla_chunked(q, k, v, g, include_diag):
    B, H, T, dk = q.shape
    dv = v.shape[-1]
    n = T // GLA_CHUNK

    def to_chunks(a):
        return jnp.moveaxis(a.reshape(B, H, n, GLA_CHUNK, a.shape[-1]), 2, 0)

    t = np.arange(GLA_CHUNK)
    mask = (t[:, None] >= t[None, :]) if include_diag else (t[:, None] > t[None, :])

    def step(state, inp):
        qi, ki, vi, gi = inp
        b = jnp.cumsum(gi, axis=-2)
        inter = jnp.einsum('bhtd,bhde->bhte', qi * jnp.exp(b), state)
        diff = b[:, :, :, None, :] - b[:, :, None, :, :]
        decay = jnp.exp(jnp.where(mask[:, :, None], diff, -jnp.inf))
        a = jnp.einsum('bhtd,bhsd,bhtsd->bhts', qi, ki, decay)
        intra = jnp.einsum('bhts,bhse->bhte', a, vi)
        b_last = b[:, :, -1:, :]
        state = (jnp.exp(b_last[:, :, 0, :])[..., None] * state
                 + jnp.einsum('bhsd,bhse->bhde', ki * jnp.exp(b_last - b), vi))
        return state, inter + intra

    s0 = jnp.zeros((B, H, dk, dv), jnp.float32)
    _, o = lax.scan(step, s0, (to_chunks(q), to_chunks(k), to_chunks(v), to_chunks(g)))
    return jnp.moveaxis(o, 0, 2).reshape(B, H, T, dv)


def hybrid_layer(x, ffn1_norm, ffn1_wg, ffn1_wu, ffn1_wd, mix_norm, w_in, na_rpb, na_gain,
                 w_gate_f, b_gate_f, w_gate_b, b_gate_b, gla_gain, w_out,
                 ffn2_norm, ffn2_wg, ffn2_wu, ffn2_wd):
    x = x + 0.5 * swiglu(rmsnorm(x, ffn1_norm), ffn1_wg, ffn1_wu, ffn1_wd)

    h = rmsnorm(x, mix_norm)
    proj = h @ w_in
    offsets = list(np.cumsum(IN_SPLITS)[:-1])
    na_q, na_k, na_v, g_q, g_k, g_v, g_r, gf_code, gb_code = jnp.split(proj, offsets, axis=-1)

    na_o = neighbourhood_attention(split_heads(na_q, NA_HEADS), split_heads(na_k, NA_HEADS),
                                   split_heads(na_v, NA_HEADS), na_rpb)
    na_o = rmsnorm(merge_heads(na_o), na_gain)

    f32 = jnp.float32
    qg = split_heads(g_q, GLA_HEADS).astype(f32) * (GLA_DK ** -0.5)
    kg = split_heads(g_k, GLA_HEADS).astype(f32)
    vg = split_heads(g_v, GLA_HEADS).astype(f32)
    log_gf = jax.nn.log_sigmoid((gf_code @ w_gate_f + b_gate_f).astype(f32)) / GLA_TAU
    log_gb = jax.nn.log_sigmoid((gb_code @ w_gate_b + b_gate_b).astype(f32)) / GLA_TAU
    log_gf = split_heads(log_gf, GLA_HEADS)
    log_gb = split_heads(log_gb, GLA_HEADS)
    o_fwd = gla_chunked(qg, kg, vg, log_gf, True)
    flip = lambda a: jnp.flip(a, axis=2)
    o_bwd = flip(gla_chunked(flip(qg), flip(kg), flip(vg), flip(log_gb), False))
    gla_o = rmsnorm(o_fwd + o_bwd, gla_gain).astype(x.dtype)
    gla_o = merge_heads(gla_o) * jax.nn.silu(g_r)

    x = x + jnp.concatenate([na_o, gla_o], axis=-1) @ w_out

    x = x + 0.5 * swiglu(rmsnorm(x, ffn2_norm), ffn2_wg, ffn2_wu, ffn2_wd)
    return x


def setup_inputs(seed: int = 0) -> dict:
    key = jax.random.key(seed)
    ks = iter(jax.random.split(key, 32))
    nrm = lambda shape, s: jax.random.normal(next(ks), shape, jnp.float32) * s
    gain = lambda shape: 1.0 + nrm(shape, 0.02)
    L, D = DEPTH, D_MODEL
    return {
        "x": nrm((BATCH, SEQ, D), 1.0),
        "ffn1_norm": gain((L, D)),
        "ffn1_wg": nrm((L, D, D_FF), D ** -0.5),
        "ffn1_wu": nrm((L, D, D_FF), D ** -0.5),
        "ffn1_wd": nrm((L, D_FF, D), D_FF ** -0.5),
        "mix_norm": gain((L, D)),
        "w_in": nrm((L, D, IN_WIDTH), D ** -0.5),
        "na_rpb": nrm((L, NA_HEADS, 2 * WIN_H_MAX - 1, 2 * WIN_W - 1), 0.1),
        "na_gain": gain((L, NA_WIDTH)),
        "w_gate_f": nrm((L, GLA_GATE_RANK, GLA_QK_WIDTH), GLA_GATE_RANK ** -0.5),
        "b_gate_f": nrm((L, GLA_QK_WIDTH), 0.1),
        "w_gate_b": nrm((L, GLA_GATE_RANK, GLA_QK_WIDTH), GLA_GATE_RANK ** -0.5),
        "b_gate_b": nrm((L, GLA_QK_WIDTH), 0.1),
        "gla_gain": gain((L, GLA_DV)),
        "w_out": nrm((L, MIX_WIDTH, D), MIX_WIDTH ** -0.5),
        "ffn2_norm": gain((L, D)),
        "ffn2_wg": nrm((L, D, D_FF), D ** -0.5),
        "ffn2_wu": nrm((L, D, D_FF), D ** -0.5),
        "ffn2_wd": nrm((L, D_FF, D), D_FF ** -0.5),
        "final_norm": gain((D,)),
    }


def reference(x, ffn1_norm, ffn1_wg, ffn1_wu, ffn1_wd, mix_norm, w_in, na_rpb, na_gain,
              w_gate_f, b_gate_f, w_gate_b, b_gate_b, gla_gain, w_out,
              ffn2_norm, ffn2_wg, ffn2_wu, ffn2_wd, final_norm):
    for l in range(DEPTH):
        x = hybrid_layer(x, ffn1_norm[l], ffn1_wg[l], ffn1_wu[l], ffn1_wd[l], mix_norm[l], w_in[l],
                         na_rpb[l], na_gain[l], w_gate_f[l], b_gate_f[l], w_gate_b[l], b_gate_b[l],
                         gla_gain[l], w_out[l], ffn2_norm[l], ffn2_wg[l], ffn2_wu[l], ffn2_wd[l])
    return rmsnorm(x, final_norm)
```

```python
import functools

import numpy as np
import jax
import jax.numpy as jnp
from jax import lax
from jax.experimental import pallas as pl
from jax.experimental.pallas import tpu as pltpu

F32 = jnp.float32
BF16 = jnp.bfloat16

EPS = 1e-6
NEG_INF = -1e30

GRID_W = 64
WIN_H = 8
WIN_W = 16
NA_HEADS = 8
NA_HEAD_DIM = 64
NA_WIDTH = NA_HEADS * NA_HEAD_DIM
GLA_HEADS = 4
GLA_DK = 64
GLA_DV = 128
GLA_QK_WIDTH = GLA_HEADS * GLA_DK
GLA_V_WIDTH = GLA_HEADS * GLA_DV
GLA_GATE_RANK = 16
GLA_TAU = 16.0
GLA_CHUNK = 64

LANES = 128
CODE_PAD = LANES
VMEM_LIMIT = 56 * 1024 * 1024

TM_FFN = 512
TM_PROJ = 512
NA_ROWS_PER_STEP = 8
GLA_BLOCK = 256


def _rmsnorm(x, g):
    return x * lax.rsqrt(jnp.mean(x * x, axis=-1, keepdims=True) + EPS) * g


def _dot(a, b):
    return jnp.dot(a, b, preferred_element_type=F32)


def _dot_nt(a, b):
    return lax.dot_general(a, b, (((1,), (1,)), ((), ())), preferred_element_type=F32)


def _const_spec(shape):
    return pl.BlockSpec(shape, lambda *_: (0,) * len(shape), pipeline_mode=pl.Buffered(1))


def _params(*sem):
    return pltpu.CompilerParams(dimension_semantics=sem, vmem_limit_bytes=VMEM_LIMIT)


def _ffn_chunks(d_ff):
    chunks, start = [], 0
    while start < d_ff:
        width = min(1024, d_ff - start)
        chunks.append((start, width))
        start += width
    return tuple(chunks)


def _swiglu_update(x, g_ref, wg_ref, wu_ref, wd_ref):
    xb = _rmsnorm(x, g_ref[...]).astype(BF16)
    y = None
    for start, width in _ffn_chunks(wg_ref.shape[1]):
        h = _dot(xb, wg_ref[:, start:start + width])
        u = _dot(xb, wu_ref[:, start:start + width])
        a = (h * jax.nn.sigmoid(h) * u).astype(BF16)
        part = _dot(a, wd_ref[start:start + width, :])
        y = part if y is None else y + part
    return x + 0.5 * y


def _ffn_body(x_ref, g_ref, wg_ref, wu_ref, wd_ref, o_ref):
    o_ref[...] = _swiglu_update(x_ref[...], g_ref, wg_ref, wu_ref, wd_ref)


def _ffn_final_body(x_ref, g_ref, wg_ref, wu_ref, wd_ref, fg_ref, o_ref):
    y = _swiglu_update(x_ref[...], g_ref, wg_ref, wu_ref, wd_ref)
    o_ref[...] = _rmsnorm(y, fg_ref[...])


def _ffn(x, norm_g, wg, wu, wd, final_g=None):
    n, d = x.shape
    d_ff = wg.shape[1]
    tile = pl.BlockSpec((TM_FFN, d), lambda i: (i, 0))
    in_specs = [tile, _const_spec((1, d)), _const_spec((d, d_ff)), _const_spec((d, d_ff)),
                _const_spec((d_ff, d))]
    args = [x, norm_g, wg, wu, wd]
    body = _ffn_body
    if final_g is not None:
        in_specs.append(_const_spec((1, d)))
        args.append(final_g)
        body = _ffn_final_body
    return pl.pallas_call(
        body,
        grid=(n // TM_FFN,),
        in_specs=in_specs,
        out_specs=tile,
        out_shape=jax.ShapeDtypeStruct((n, d), F32),
        compiler_params=_params("parallel"),
        name="ffn_final" if final_g is not None else "ffn",
    )(*args)


def _log_sigmoid(x):
    return jnp.minimum(x, 0.0) - jnp.log1p(jnp.exp(-jnp.abs(x)))


def _inproj_body(x_ref, g_ref, wm_ref, wc_ref, wgate_ref, bgate_ref,
                 naq_ref, nak_ref, nav_ref, gq_ref, gk_ref, gv_ref, gr_ref, lg_ref):
    xb = _rmsnorm(x_ref[...], g_ref[...]).astype(BF16)
    col = 0

    def seg(width):
        nonlocal col
        out = _dot(xb, wm_ref[:, col:col + width])
        col += width
        return out

    naq_ref[...] = (seg(NA_WIDTH) * (NA_HEAD_DIM ** -0.5)).astype(BF16)
    nak_ref[...] = seg(NA_WIDTH).astype(BF16)
    nav_ref[...] = seg(NA_WIDTH).astype(BF16)
    gq_ref[...] = seg(GLA_QK_WIDTH) * (GLA_DK ** -0.5)
    gk_ref[...] = seg(GLA_QK_WIDTH)
    gv_ref[...] = seg(GLA_V_WIDTH).astype(BF16)
    gr_ref[...] = seg(GLA_V_WIDTH)
    codes = _dot(xb, wc_ref[...])
    logits = jnp.dot(codes, wgate_ref[...], precision=lax.Precision.HIGHEST,
                     preferred_element_type=F32) + bgate_ref[...]
    lg_ref[...] = _log_sigmoid(logits) * (1.0 / GLA_TAU)


def _inproj(x, norm_g, w_main, w_code, w_gate, b_gate):
    n, d = x.shape
    tm = TM_PROJ

    def tile(width):
        return pl.BlockSpec((tm, width), lambda i: (i, 0))

    widths = (NA_WIDTH, NA_WIDTH, NA_WIDTH, GLA_QK_WIDTH, GLA_QK_WIDTH, GLA_V_WIDTH,
              GLA_V_WIDTH, 2 * GLA_QK_WIDTH)
    dtypes = (BF16, BF16, BF16, F32, F32, BF16, F32, F32)
    return pl.pallas_call(
        _inproj_body,
        grid=(n // tm,),
        in_specs=[tile(d), _const_spec((1, d)), _const_spec(w_main.shape),
                  _const_spec(w_code.shape), _const_spec(w_gate.shape), _const_spec(b_gate.shape)],
        out_specs=[tile(w) for w in widths],
        out_shape=[jax.ShapeDtypeStruct((n, w), dt) for w, dt in zip(widths, dtypes)],
        compiler_params=_params("parallel"),
        name="inproj",
    )(x, norm_g, w_main, w_code, w_gate, b_gate)


_NA_PATTERNS = 8


def _na_bias_table(rpb, nrows):
    half = WIN_H // 2
    pat_rows = np.array(list(range(half)) + [half] + list(range(nrows - half + 1, nrows)))
    row_start = np.clip(pat_rows - half, 0, nrows - WIN_H)
    dr = row_start[:, None] + np.arange(WIN_H)[None, :] - pat_rows[:, None] + (WIN_H - 1)
    qc = np.arange(GRID_W)
    kc = np.arange(GRID_W)
    col_start = np.clip(qc - WIN_W // 2, 0, GRID_W - WIN_W)
    valid = (kc[None, :] >= col_start[:, None]) & (kc[None, :] < col_start[:, None] + WIN_W)
    dc = np.clip(kc[None, :] - qc[:, None] + (WIN_W - 1), 0, 2 * WIN_W - 2)
    bias = rpb[:, dr[:, None, :, None], dc[None, :, None, :]]
    bias = jnp.where(valid[None, None, :, None, :], bias.astype(F32), NEG_INF)
    return bias.reshape(rpb.shape[0], _NA_PATTERNS, GRID_W, WIN_H * GRID_W)


def _na_body(q_ref, k_ref, v_ref, bias_ref, o_ref, *, nrows):
    rb = pl.program_id(2)
    half = WIN_H // 2
    lane = lax.broadcasted_iota(jnp.int32, (GRID_W, LANES), 1)
    first_head = lane < NA_HEAD_DIM
    nkeys = WIN_H * GRID_W

    def row_body(i, carry):
        r = rb * NA_ROWS_PER_STEP + i
        row_start = jnp.clip(r - half, 0, nrows - WIN_H)
        pat = jnp.where(r < half, r, jnp.where(r > nrows - half, r - (nrows - WIN_H), half))
        koff = pl.multiple_of(row_start * GRID_W, GRID_W)
        qoff = pl.multiple_of(i * GRID_W, GRID_W)
        ks = k_ref[0, pl.ds(koff, nkeys), :]
        vs = v_ref[0, pl.ds(koff, nkeys), :]
        qr = q_ref[0, pl.ds(qoff, GRID_W), :]
        outs = []
        for hl in range(2):
            mask = first_head if hl == 0 else jnp.logical_not(first_head)
            qm = jnp.where(mask, qr, jnp.zeros_like(qr))
            s = _dot_nt(qm, ks) + bias_ref[hl, pat]
            m = jnp.max(s, axis=-1, keepdims=True)
            e = jnp.exp(s - m)
            l = jnp.sum(e, axis=-1, keepdims=True)
            outs.append(_dot(e.astype(BF16), vs) / l)
        o_ref[0, pl.ds(qoff, GRID_W), :] = jnp.where(first_head, outs[0], outs[1])
        return carry

    lax.fori_loop(0, NA_ROWS_PER_STEP, row_body, 0)


def _na(q, k, v, bias):
    b, s, _ = q.shape
    nrows = s // GRID_W
    qrows = NA_ROWS_PER_STEP * GRID_W
    return pl.pallas_call(
        functools.partial(_na_body, nrows=nrows),
        grid=(NA_HEADS // 2, b, nrows // NA_ROWS_PER_STEP),
        in_specs=[
            pl.BlockSpec((1, qrows, LANES), lambda hp, bi, rb: (bi, rb, hp)),
            pl.BlockSpec((1, s, LANES), lambda hp, bi, rb: (bi, 0, hp)),
            pl.BlockSpec((1, s, LANES), lambda hp, bi, rb: (bi, 0, hp)),
            pl.BlockSpec((2, _NA_PATTERNS, GRID_W, WIN_H * GRID_W), lambda hp, bi, rb: (hp, 0, 0, 0)),
        ],
        out_specs=pl.BlockSpec((1, qrows, LANES), lambda hp, bi, rb: (bi, rb, hp)),
        out_shape=jax.ShapeDtypeStruct((b, s, NA_WIDTH), F32),
        compiler_params=_params("parallel", "parallel", "arbitrary"),
        name="na",
    )(q, k, v, bias)


def _split3(x):
    x1 = x.astype(BF16)
    r1 = x - x1.astype(F32)
    x2 = r1.astype(BF16)
    x3 = (r1 - x2.astype(F32)).astype(BF16)
    return x1, x2, x3


def _gla_direction(q_ref, k_ref, v_ref, g_ref, o_ref, s_ref, *, tri, a_mask, last, order,
                   head_masks, chunk_masks):
    nchunk = GLA_BLOCK // GLA_CHUNK
    g = g_ref[0]
    b = sum(_dot(tri, part) for part in _split3(g))
    b_last = jnp.concatenate(
        [jnp.broadcast_to(b[c * GLA_CHUNK + last:c * GLA_CHUNK + last + 1, :],
                          (GLA_CHUNK, GLA_QK_WIDTH)) for c in range(nchunk)], axis=0)
    q = q_ref[0]
    k = k_ref[0]
    v = v_ref[0]
    qt = (q * jnp.exp(b)).astype(BF16)
    kt = (k * jnp.exp(-b)).astype(BF16)
    ku_t = (k * jnp.exp(b_last - b)).T.astype(BF16)
    decay_t = jnp.exp(b_last.T)
    for c in order:
        rows = slice(c * GLA_CHUNK, (c + 1) * GLA_CHUNK)
        qc = qt[rows]
        qs = jnp.concatenate([jnp.where(hm, qc, jnp.zeros_like(qc)) for hm in head_masks], axis=0)
        a = _dot_nt(qs, kt[rows])
        a = jnp.where(a_mask, a, 0.0).astype(BF16)
        inter = _dot(qs, s_ref[...].astype(BF16))
        ku_c = jnp.where(chunk_masks[c], ku_t, jnp.zeros_like(ku_t))
        col = c * GLA_CHUNK
        for h in range(GLA_HEADS):
            hrows = slice(h * GLA_DK, (h + 1) * GLA_DK)
            arows = slice(h * GLA_CHUNK, (h + 1) * GLA_CHUNK)
            vcols = slice(h * GLA_DV, (h + 1) * GLA_DV)
            intra = _dot(a[arows], v[rows, vcols])
            o_ref[0, rows, vcols] = inter[arows] + intra
            u = _dot(ku_c[hrows], v[:, vcols])
            s_ref[hrows, :] = decay_t[hrows, col:col + 1] * s_ref[hrows, :] + u


def _gla_body(qf_ref, kf_ref, vf_ref, gf_ref, qb_ref, kb_ref, vb_ref, gb_ref,
              of_ref, ob_ref, sf_ref, sb_ref):
    @pl.when(pl.program_id(1) == 0)
    def _():
        sf_ref[...] = jnp.zeros_like(sf_ref)
        sb_ref[...] = jnp.zeros_like(sb_ref)

    shift = GLA_CHUNK.bit_length() - 1
    row = lax.broadcasted_iota(jnp.int32, (GLA_BLOCK, GLA_BLOCK), 0)
    col = lax.broadcasted_iota(jnp.int32, (GLA_BLOCK, GLA_BLOCK), 1)
    same_chunk = (row >> shift) == (col >> shift)
    lower = jnp.where(same_chunk & (row >= col), 1.0, 0.0).astype(BF16)
    upper = jnp.where(same_chunk & (row <= col), 1.0, 0.0).astype(BF16)
    nchunk = GLA_BLOCK // GLA_CHUNK
    lane = lax.broadcasted_iota(jnp.int32, (GLA_CHUNK, GLA_QK_WIDTH), 1)
    head_masks = [(lane >> shift) == h for h in range(GLA_HEADS)]
    chunk_masks = [(col >> shift) == c for c in range(nchunk)]
    t = lax.broadcasted_iota(jnp.int32, (GLA_HEADS * GLA_CHUNK, GLA_CHUNK), 0) & (GLA_CHUNK - 1)
    s = lax.broadcasted_iota(jnp.int32, (GLA_HEADS * GLA_CHUNK, GLA_CHUNK), 1)
    common = dict(head_masks=head_masks, chunk_masks=chunk_masks)
    _gla_direction(qf_ref, kf_ref, vf_ref, gf_ref, of_ref, sf_ref, tri=lower, a_mask=t >= s,
                   last=GLA_CHUNK - 1, order=tuple(range(nchunk)), **common)
    _gla_direction(qb_ref, kb_ref, vb_ref, gb_ref, ob_ref, sb_ref, tri=upper, a_mask=t < s,
                   last=0, order=tuple(reversed(range(nchunk))), **common)


def _gla(q, k, v, lg):
    b, s, _ = q.shape
    nb = s // GLA_BLOCK

    def fwd(width, j=0):
        return pl.BlockSpec((1, GLA_BLOCK, width), lambda bi, n: (bi, n, j))

    def bwd(width, j=0):
        return pl.BlockSpec((1, GLA_BLOCK, width), lambda bi, n: (bi, nb - 1 - n, j))

    state = pltpu.VMEM((GLA_QK_WIDTH, GLA_DV), F32)
    return pl.pallas_call(
        _gla_body,
        grid=(b, nb),
        in_specs=[fwd(GLA_QK_WIDTH), fwd(GLA_QK_WIDTH), fwd(GLA_V_WIDTH), fwd(GLA_QK_WIDTH, 0),
                  bwd(GLA_QK_WIDTH), bwd(GLA_QK_WIDTH), bwd(GLA_V_WIDTH), bwd(GLA_QK_WIDTH, 1)],
        out_specs=[fwd(GLA_V_WIDTH), bwd(GLA_V_WIDTH)],
        out_shape=[jax.ShapeDtypeStruct((b, s, GLA_V_WIDTH), F32)] * 2,
        scratch_shapes=[state, state],
        compiler_params=_params("parallel", "arbitrary"),
        name="gla",
    )(q, k, v, lg, q, k, v, lg)


def _outproj_body(x_ref, na_ref, nag_ref, of_ref, ob_ref, gg_ref, gr_ref, wna_ref, wgla_ref, o_ref):
    na = _rmsnorm(na_ref[...], nag_ref[...]).astype(BF16)
    y = _dot(na, wna_ref[...])
    o = of_ref[...] + ob_ref[...]
    r = gr_ref[...]
    gate = r * jax.nn.sigmoid(r)
    heads = []
    for h in range(GLA_HEADS):
        cols = slice(h * GLA_DV, (h + 1) * GLA_DV)
        heads.append((_rmsnorm(o[:, cols], gg_ref[...]) * gate[:, cols]).astype(BF16))
    y = y + _dot(jnp.concatenate(heads, axis=-1), wgla_ref[...])
    o_ref[...] = x_ref[...] + y


def _outproj(x, na_o, na_gain, o_f, o_b, gla_gain, g_r, w_na, w_gla):
    n, d = x.shape
    tm = TM_PROJ

    def tile(width):
        return pl.BlockSpec((tm, width), lambda i: (i, 0))

    return pl.pallas_call(
        _outproj_body,
        grid=(n // tm,),
        in_specs=[tile(d), tile(NA_WIDTH), _const_spec((1, NA_WIDTH)), tile(GLA_V_WIDTH),
                  tile(GLA_V_WIDTH), _const_spec((1, GLA_DV)), tile(GLA_V_WIDTH),
                  _const_spec(w_na.shape), _const_spec(w_gla.shape)],
        out_specs=tile(d),
        out_shape=jax.ShapeDtypeStruct((n, d), F32),
        compiler_params=_params("parallel"),
        name="outproj",
    )(x, na_o, na_gain, o_f, o_b, gla_gain, g_r, w_na, w_gla)


def _gate_weights(w_f, b_f, w_b, b_b):
    w = jnp.zeros((CODE_PAD, 2 * GLA_QK_WIDTH), F32)
    w = w.at[:GLA_GATE_RANK, :GLA_QK_WIDTH].set(w_f)
    w = w.at[GLA_GATE_RANK:2 * GLA_GATE_RANK, GLA_QK_WIDTH:].set(w_b)
    return w, jnp.concatenate([b_f, b_b])[None, :]


def kernel(x, ffn1_norm, ffn1_wg, ffn1_wu, ffn1_wd, mix_norm, w_in, na_rpb, na_gain, w_gate_f, b_gate_f, w_gate_b, b_gate_b, gla_gain, w_out, ffn2_norm, ffn2_wg, ffn2_wu, ffn2_wd, final_norm):
    bsz, seq, d = x.shape
    depth = ffn1_norm.shape[0]
    n = bsz * seq
    main_width = 3 * NA_WIDTH + 2 * GLA_QK_WIDTH + 2 * GLA_V_WIDTH
    xt = x.reshape(n, d)
    for l in range(depth):
        xt = _ffn(xt, ffn1_norm[l][None], ffn1_wg[l].astype(BF16), ffn1_wu[l].astype(BF16),
                  ffn1_wd[l].astype(BF16))
        w_main = w_in[l][:, :main_width].astype(BF16)
        w_code = jnp.pad(w_in[l][:, main_width:], ((0, 0), (0, CODE_PAD - 2 * GLA_GATE_RANK))).astype(BF16)
        w_gate, b_gate = _gate_weights(w_gate_f[l], b_gate_f[l], w_gate_b[l], b_gate_b[l])
        na_q, na_k, na_v, g_q, g_k, g_v, g_r, lg = _inproj(xt, mix_norm[l][None], w_main, w_code,
                                                           w_gate, b_gate)
        to3 = lambda a: a.reshape(bsz, seq, a.shape[-1])
        na_o = _na(to3(na_q), to3(na_k), to3(na_v), _na_bias_table(na_rpb[l], seq // GRID_W))
        o_f, o_b = _gla(to3(g_q), to3(g_k), to3(g_v), to3(lg))
        w_o = w_out[l].astype(BF16)
        xt = _outproj(xt, na_o.reshape(n, NA_WIDTH), na_gain[l][None], o_f.reshape(n, GLA_V_WIDTH),
                      o_b.reshape(n, GLA_V_WIDTH), gla_gain[l][None], g_r, w_o[:NA_WIDTH], w_o[NA_WIDTH:])
        xt = _ffn(xt, ffn2_norm[l][None], ffn2_wg[l].astype(BF16), ffn2_wu[l].astype(BF16),
                  ffn2_wd[l].astype(BF16), final_g=final_norm[None] if l == depth - 1 else None)
    return xt.reshape(bsz, seq, d)
```

```python
import functools

import numpy as np
import jax
import jax.numpy as jnp
from jax import lax
from jax.experimental import pallas as pl
from jax.experimental.pallas import tpu as pltpu

F32 = jnp.float32
BF16 = jnp.bfloat16

EPS = 1e-6
NEG_INF = -1e30

GRID_W = 64
WIN_H = 8
WIN_W = 16
NA_HEADS = 8
NA_HEAD_DIM = 64
NA_WIDTH = NA_HEADS * NA_HEAD_DIM
GLA_HEADS = 4
GLA_DK = 64
GLA_DV = 128
GLA_QK_WIDTH = GLA_HEADS * GLA_DK
GLA_V_WIDTH = GLA_HEADS * GLA_DV
GLA_GATE_RANK = 16
GLA_TAU = 16.0
GLA_CHUNK = 64

LANES = 128
CODE_PAD = LANES
VMEM_LIMIT = 56 * 1024 * 1024

TM_FFN = 512
TM_PROJ = 512
NA_ROWS_PER_STEP = 16
NA_LOOKAHEAD = 3
GLA_BLOCK = 256


def _rmsnorm(x, g):
    return x * lax.rsqrt(jnp.mean(x * x, axis=-1, keepdims=True) + EPS) * g


def _dot(a, b):
    return jnp.dot(a, b, preferred_element_type=F32)


def _dot_nt(a, b):
    return lax.dot_general(a, b, (((1,), (1,)), ((), ())), preferred_element_type=F32)


def _const_spec(shape):
    return pl.BlockSpec(shape, lambda *_: (0,) * len(shape), pipeline_mode=pl.Buffered(1))


def _params(*sem):
    return pltpu.CompilerParams(dimension_semantics=sem, vmem_limit_bytes=VMEM_LIMIT)


def _ffn_chunks(d_ff):
    chunks, start = [], 0
    while start < d_ff:
        width = min(1024, d_ff - start)
        chunks.append((start, width))
        start += width
    return tuple(chunks)


def _swiglu_update(x, g_ref, wg_ref, wu_ref, wd_ref):
    xb = _rmsnorm(x, g_ref[...]).astype(BF16)
    y = None
    for start, width in _ffn_chunks(wg_ref.shape[1]):
        h = _dot(xb, wg_ref[:, start:start + width])
        u = _dot(xb, wu_ref[:, start:start + width])
        a = (h * jax.nn.sigmoid(h) * u).astype(BF16)
        part = _dot(a, wd_ref[start:start + width, :])
        y = part if y is None else y + part
    return x + 0.5 * y


def _ffn_body(x_ref, g_ref, wg_ref, wu_ref, wd_ref, o_ref):
    o_ref[...] = _swiglu_update(x_ref[...], g_ref, wg_ref, wu_ref, wd_ref)


def _ffn_final_body(x_ref, g_ref, wg_ref, wu_ref, wd_ref, fg_ref, o_ref):
    y = _swiglu_update(x_ref[...], g_ref, wg_ref, wu_ref, wd_ref)
    o_ref[...] = _rmsnorm(y, fg_ref[...])


def _ffn(x, norm_g, wg, wu, wd, final_g=None):
    n, d = x.shape
    d_ff = wg.shape[1]
    tile = pl.BlockSpec((TM_FFN, d), lambda i: (i, 0))
    in_specs = [tile, _const_spec((1, d)), _const_spec((d, d_ff)), _const_spec((d, d_ff)),
                _const_spec((d_ff, d))]
    args = [x, norm_g, wg, wu, wd]
    body = _ffn_body
    if final_g is not None:
        in_specs.append(_const_spec((1, d)))
        args.append(final_g)
        body = _ffn_final_body
    return pl.pallas_call(
        body,
        grid=(n // TM_FFN,),
        in_specs=in_specs,
        out_specs=tile,
        out_shape=jax.ShapeDtypeStruct((n, d), F32),
        compiler_params=_params("parallel"),
        name="ffn_final" if final_g is not None else "ffn",
    )(*args)


def _log_sigmoid(x):
    return jnp.minimum(x, 0.0) - jnp.log1p(jnp.exp(-jnp.abs(x)))


def _inproj_body(x_ref, g_ref, wm_ref, wc_ref, wgate_ref, bgate_ref,
                 naq_ref, nak_ref, nav_ref, gq_ref, gk_ref, gv_ref, gr_ref, lg_ref):
    xb = _rmsnorm(x_ref[...], g_ref[...]).astype(BF16)
    col = 0

    def seg(width):
        nonlocal col
        out = _dot(xb, wm_ref[:, col:col + width])
        col += width
        return out

    naq_ref[...] = (seg(NA_WIDTH) * (NA_HEAD_DIM ** -0.5)).astype(BF16)
    nak_ref[...] = seg(NA_WIDTH).astype(BF16)
    nav_ref[...] = seg(NA_WIDTH).astype(BF16)
    gq_ref[...] = seg(GLA_QK_WIDTH) * (GLA_DK ** -0.5)
    gk_ref[...] = seg(GLA_QK_WIDTH)
    gv_ref[...] = seg(GLA_V_WIDTH).astype(BF16)
    gr_ref[...] = seg(GLA_V_WIDTH)
    codes = _dot(xb, wc_ref[...])
    logits = _dot(codes.astype(BF16), wgate_ref[...].astype(BF16)) + bgate_ref[...]
    lg_ref[...] = _log_sigmoid(logits) * (1.0 / GLA_TAU)


def _inproj(x, norm_g, w_main, w_code, w_gate, b_gate):
    n, d = x.shape
    tm = TM_PROJ

    def tile(width):
        return pl.BlockSpec((tm, width), lambda i: (i, 0))

    widths = (NA_WIDTH, NA_WIDTH, NA_WIDTH, GLA_QK_WIDTH, GLA_QK_WIDTH, GLA_V_WIDTH,
              GLA_V_WIDTH, 2 * GLA_QK_WIDTH)
    dtypes = (BF16, BF16, BF16, F32, F32, BF16, F32, F32)
    return pl.pallas_call(
        _inproj_body,
        grid=(n // tm,),
        in_specs=[tile(d), _const_spec((1, d)), _const_spec(w_main.shape),
                  _const_spec(w_code.shape), _const_spec(w_gate.shape), _const_spec(b_gate.shape)],
        out_specs=[tile(w) for w in widths],
        out_shape=[jax.ShapeDtypeStruct((n, w), dt) for w, dt in zip(widths, dtypes)],
        compiler_params=_params("parallel"),
        name="inproj",
    )(x, norm_g, w_main, w_code, w_gate, b_gate)


_NA_ROW_PAIRS = 2 * WIN_H - 2


def _na_bias_table(rpb):
    qc = np.arange(GRID_W)
    kc = np.arange(GRID_W)
    col_start = np.clip(qc - WIN_W // 2, 0, GRID_W - WIN_W)
    valid = (kc[None, :] >= col_start[:, None]) & (kc[None, :] < col_start[:, None] + WIN_W)
    dc = np.clip(kc[None, :] - qc[:, None] + (WIN_W - 1), 0, 2 * WIN_W - 2)
    onehot = (dc[:, :, None] == np.arange(2 * WIN_W - 1)[None, None, :]).astype(np.float32)
    t = jnp.einsum("hrc,qkc->hrqk", rpb.astype(F32), onehot, precision=lax.Precision.HIGHEST)
    t = jnp.where(valid[None, None], t, NEG_INF)
    return jnp.concatenate([t[:, :-1], t[:, 1:]], axis=-1)


def _na_body(q_ref, k_ref, v_ref, bias_ref, o_ref, s_ref, *, nrows):
    rb = pl.program_id(2)
    half = WIN_H // 2
    lane = lax.broadcasted_iota(jnp.int32, (GRID_W, LANES), 1)
    first_head = lane < NA_HEAD_DIM
    nkeys = WIN_H * GRID_W

    def key_offset(i):
        r = rb * NA_ROWS_PER_STEP + i
        row_start = jnp.clip(r - half, 0, nrows - WIN_H)
        return r, row_start, pl.multiple_of(row_start * GRID_W, GRID_W)

    def scores(i):
        _, _, koff = key_offset(i)
        qr = q_ref[0, i * GRID_W:(i + 1) * GRID_W, :]
        zero = jnp.zeros_like(qr)
        qs = jnp.concatenate([jnp.where(first_head, qr, zero), jnp.where(first_head, zero, qr)], axis=0)
        return _dot_nt(qs, k_ref[0, pl.ds(koff, nkeys), :])

    def finish(i):
        r, row_start, koff = key_offset(i)
        dr0 = row_start - r + (WIN_H - 1)
        bias = jnp.concatenate(
            [jnp.concatenate([bias_ref[hl, dr0 + 2 * wp] for wp in range(WIN_H // 2)], axis=1)
             for hl in range(2)], axis=0)
        s = s_ref[i % (NA_LOOKAHEAD + 1)] + bias
        e = jnp.exp(s - jnp.max(s, axis=-1, keepdims=True))
        o = _dot(e.astype(BF16), v_ref[0, pl.ds(koff, nkeys), :])
        o = o / jnp.sum(e, axis=-1, keepdims=True)
        o_ref[0, i * GRID_W:(i + 1) * GRID_W, :] = jnp.where(first_head, o[:GRID_W], o[GRID_W:])

    for i in range(NA_LOOKAHEAD):
        s_ref[i] = scores(i)
    for i in range(NA_ROWS_PER_STEP):
        ahead = i + NA_LOOKAHEAD
        if ahead < NA_ROWS_PER_STEP:
            s_ref[ahead % (NA_LOOKAHEAD + 1)] = scores(ahead)
        finish(i)


def _na(q, k, v, bias):
    b, s, _ = q.shape
    nrows = s // GRID_W
    qrows = NA_ROWS_PER_STEP * GRID_W
    return pl.pallas_call(
        functools.partial(_na_body, nrows=nrows),
        grid=(NA_HEADS // 2, b, nrows // NA_ROWS_PER_STEP),
        in_specs=[
            pl.BlockSpec((1, qrows, LANES), lambda hp, bi, rb: (bi, rb, hp)),
            pl.BlockSpec((1, s, LANES), lambda hp, bi, rb: (bi, 0, hp)),
            pl.BlockSpec((1, s, LANES), lambda hp, bi, rb: (bi, 0, hp)),
            pl.BlockSpec((2, _NA_ROW_PAIRS, GRID_W, LANES), lambda hp, bi, rb: (hp, 0, 0, 0)),
        ],
        out_specs=pl.BlockSpec((1, qrows, LANES), lambda hp, bi, rb: (bi, rb, hp)),
        out_shape=jax.ShapeDtypeStruct((b, s, NA_WIDTH), F32),
        scratch_shapes=[pltpu.VMEM((NA_LOOKAHEAD + 1, 2 * GRID_W, WIN_H * GRID_W), F32)],
        compiler_params=_params("parallel", "parallel", "arbitrary"),
        name="na",
    )(q, k, v, bias)


def _split3(x):
    x1 = x.astype(BF16)
    r1 = x - x1.astype(F32)
    x2 = r1.astype(BF16)
    x3 = (r1 - x2.astype(F32)).astype(BF16)
    return x1, x2, x3


def _gla_direction(q_ref, k_ref, v_ref, g_ref, o_ref, s_ref, *, tri, a_mask, last, order,
                   head_masks, chunk_masks):
    nchunk = GLA_BLOCK // GLA_CHUNK
    g = g_ref[0]
    b = sum(_dot(tri, part) for part in _split3(g))
    b_last = jnp.concatenate(
        [jnp.broadcast_to(b[c * GLA_CHUNK + last:c * GLA_CHUNK + last + 1, :],
                          (GLA_CHUNK, GLA_QK_WIDTH)) for c in range(nchunk)], axis=0)
    q = q_ref[0]
    k = k_ref[0]
    v = v_ref[0]
    qt = (q * jnp.exp(b)).astype(BF16)
    kt = (k * jnp.exp(-b)).astype(BF16)
    ku_t = (k * jnp.exp(b_last - b)).T.astype(BF16)
    decay_t = jnp.exp(b_last.T)
    for c in order:
        rows = slice(c * GLA_CHUNK, (c + 1) * GLA_CHUNK)
        qc = qt[rows]
        qs = jnp.concatenate([jnp.where(hm, qc, jnp.zeros_like(qc)) for hm in head_masks], axis=0)
        a = _dot_nt(qs, kt[rows])
        a = jnp.where(a_mask, a, 0.0).astype(BF16)
        inter = _dot(qs, s_ref[...].astype(BF16))
        ku_c = jnp.where(chunk_masks[c], ku_t, jnp.zeros_like(ku_t))
        col = c * GLA_CHUNK
        for h in range(GLA_HEADS):
            hrows = slice(h * GLA_DK, (h + 1) * GLA_DK)
            arows = slice(h * GLA_CHUNK, (h + 1) * GLA_CHUNK)
            vcols = slice(h * GLA_DV, (h + 1) * GLA_DV)
            intra = _dot(a[arows], v[rows, vcols])
            o_ref[0, rows, vcols] = inter[arows] + intra
            u = _dot(ku_c[hrows], v[:, vcols])
            s_ref[hrows, :] = decay_t[hrows, col:col + 1] * s_ref[hrows, :] + u


def _gla_body(qf_ref, kf_ref, vf_ref, gf_ref, qb_ref, kb_ref, vb_ref, gb_ref,
              of_ref, ob_ref, sf_ref, sb_ref):
    @pl.when(pl.program_id(1) == 0)
    def _():
        sf_ref[...] = jnp.zeros_like(sf_ref)
        sb_ref[...] = jnp.zeros_like(sb_ref)

    shift = GLA_CHUNK.bit_length() - 1
    row = lax.broadcasted_iota(jnp.int32, (GLA_BLOCK, GLA_BLOCK), 0)
    col = lax.broadcasted_iota(jnp.int32, (GLA_BLOCK, GLA_BLOCK), 1)
    same_chunk = (row >> shift) == (col >> shift)
    lower = jnp.where(same_chunk & (row >= col), 1.0, 0.0).astype(BF16)
    upper = jnp.where(same_chunk & (row <= col), 1.0, 0.0).astype(BF16)
    nchunk = GLA_BLOCK // GLA_CHUNK
    lane = lax.broadcasted_iota(jnp.int32, (GLA_CHUNK, GLA_QK_WIDTH), 1)
    head_masks = [(lane >> shift) == h for h in range(GLA_HEADS)]
    chunk_masks = [(col >> shift) == c for c in range(nchunk)]
    t = lax.broadcasted_iota(jnp.int32, (GLA_HEADS * GLA_CHUNK, GLA_CHUNK), 0) & (GLA_CHUNK - 1)
    s = lax.broadcasted_iota(jnp.int32, (GLA_HEADS * GLA_CHUNK, GLA_CHUNK), 1)
    common = dict(head_masks=head_masks, chunk_masks=chunk_masks)
    _gla_direction(qf_ref, kf_ref, vf_ref, gf_ref, of_ref, sf_ref, tri=lower, a_mask=t >= s,
                   last=GLA_CHUNK - 1, order=tuple(range(nchunk)), **common)
    _gla_direction(qb_ref, kb_ref, vb_ref, gb_ref, ob_ref, sb_ref, tri=upper, a_mask=t < s,
                   last=0, order=tuple(reversed(range(nchunk))), **common)


def _gla(q, k, v, lg):
    b, s, _ = q.shape
    nb = s // GLA_BLOCK

    def fwd(width, j=0):
        return pl.BlockSpec((1, GLA_BLOCK, width), lambda bi, n: (bi, n, j))

    def bwd(width, j=0):
        return pl.BlockSpec((1, GLA_BLOCK, width), lambda bi, n: (bi, nb - 1 - n, j))

    state = pltpu.VMEM((GLA_QK_WIDTH, GLA_DV), F32)
    return pl.pallas_call(
        _gla_body,
        grid=(b, nb),
        in_specs=[fwd(GLA_QK_WIDTH), fwd(GLA_QK_WIDTH), fwd(GLA_V_WIDTH), fwd(GLA_QK_WIDTH, 0),
                  bwd(GLA_QK_WIDTH), bwd(GLA_QK_WIDTH), bwd(GLA_V_WIDTH), bwd(GLA_QK_WIDTH, 1)],
        out_specs=[fwd(GLA_V_WIDTH), bwd(GLA_V_WIDTH)],
        out_shape=[jax.ShapeDtypeStruct((b, s, GLA_V_WIDTH), F32)] * 2,
        scratch_shapes=[state, state],
        compiler_params=_params("parallel", "arbitrary"),
        name="gla",
    )(q, k, v, lg, q, k, v, lg)


def _outproj_body(x_ref, na_ref, nag_ref, of_ref, ob_ref, gg_ref, gr_ref, wna_ref, wgla_ref, o_ref):
    na = _rmsnorm(na_ref[...], nag_ref[...]).astype(BF16)
    y = _dot(na, wna_ref[...])
    o = of_ref[...] + ob_ref[...]
    r = gr_ref[...]
    gate = r * jax.nn.sigmoid(r)
    heads = []
    for h in range(GLA_HEADS):
        cols = slice(h * GLA_DV, (h + 1) * GLA_DV)
        heads.append((_rmsnorm(o[:, cols], gg_ref[...]) * gate[:, cols]).astype(BF16))
    y = y + _dot(jnp.concatenate(heads, axis=-1), wgla_ref[...])
    o_ref[...] = x_ref[...] + y


def _outproj(x, na_o, na_gain, o_f, o_b, gla_gain, g_r, w_na, w_gla):
    n, d = x.shape
    tm = TM_PROJ

    def tile(width):
        return pl.BlockSpec((tm, width), lambda i: (i, 0))

    return pl.pallas_call(
        _outproj_body,
        grid=(n // tm,),
        in_specs=[tile(d), tile(NA_WIDTH), _const_spec((1, NA_WIDTH)), tile(GLA_V_WIDTH),
                  tile(GLA_V_WIDTH), _const_spec((1, GLA_DV)), tile(GLA_V_WIDTH),
                  _const_spec(w_na.shape), _const_spec(w_gla.shape)],
        out_specs=tile(d),
        out_shape=jax.ShapeDtypeStruct((n, d), F32),
        compiler_params=_params("parallel"),
        name="outproj",
    )(x, na_o, na_gain, o_f, o_b, gla_gain, g_r, w_na, w_gla)


def _gate_weights(w_f, b_f, w_b, b_b):
    w = jnp.zeros((CODE_PAD, 2 * GLA_QK_WIDTH), F32)
    w = w.at[:GLA_GATE_RANK, :GLA_QK_WIDTH].set(w_f)
    w = w.at[GLA_GATE_RANK:2 * GLA_GATE_RANK, GLA_QK_WIDTH:].set(w_b)
    return w, jnp.concatenate([b_f, b_b])[None, :]


def kernel(x, ffn1_norm, ffn1_wg, ffn1_wu, ffn1_wd, mix_norm, w_in, na_rpb, na_gain, w_gate_f, b_gate_f, w_gate_b, b_gate_b, gla_gain, w_out, ffn2_norm, ffn2_wg, ffn2_wu, ffn2_wd, final_norm):
    bsz, seq, d = x.shape
    depth = ffn1_norm.shape[0]
    n = bsz * seq
    main_width = 3 * NA_WIDTH + 2 * GLA_QK_WIDTH + 2 * GLA_V_WIDTH
    xt = x.reshape(n, d)
    for l in range(depth):
        xt = _ffn(xt, ffn1_norm[l][None], ffn1_wg[l].astype(BF16), ffn1_wu[l].astype(BF16),
                  ffn1_wd[l].astype(BF16))
        w_main = w_in[l][:, :main_width].astype(BF16)
        w_code = jnp.pad(w_in[l][:, main_width:], ((0, 0), (0, CODE_PAD - 2 * GLA_GATE_RANK))).astype(BF16)
        w_gate, b_gate = _gate_weights(w_gate_f[l], b_gate_f[l], w_gate_b[l], b_gate_b[l])
        na_q, na_k, na_v, g_q, g_k, g_v, g_r, lg = _inproj(xt, mix_norm[l][None], w_main, w_code,
                                                           w_gate, b_gate)
        to3 = lambda a: a.reshape(bsz, seq, a.shape[-1])
        na_o = _na(to3(na_q), to3(na_k), to3(na_v), _na_bias_table(na_rpb[l]))
        o_f, o_b = _gla(to3(g_q), to3(g_k), to3(g_v), to3(lg))
        w_o = w_out[l].astype(BF16)
        xt = _outproj(xt, na_o.reshape(n, NA_WIDTH), na_gain[l][None], o_f.reshape(n, GLA_V_WIDTH),
                      o_b.reshape(n, GLA_V_WIDTH), gla_gain[l][None], g_r, w_o[:NA_WIDTH], w_o[NA_WIDTH:])
        xt = _ffn(xt, ffn2_norm[l][None], ffn2_wg[l].astype(BF16), ffn2_wu[l].astype(BF16),
                  ffn2_wd[l].astype(BF16), final_g=final_norm[None] if l == depth - 1 else None)
    return xt.reshape(bsz, seq, d)
```

```python
import functools

import numpy as np
import jax
import jax.numpy as jnp
from jax import lax
from jax.experimental import pallas as pl
from jax.experimental.pallas import tpu as pltpu

F32 = jnp.float32
BF16 = jnp.bfloat16

EPS = 1e-6
NEG_INF = -1e30

GRID_W = 64
WIN_H = 8
WIN_W = 16
NA_HEADS = 8
NA_HEAD_DIM = 64
NA_WIDTH = NA_HEADS * NA_HEAD_DIM
GLA_HEADS = 4
GLA_DK = 64
GLA_DV = 128
GLA_QK_WIDTH = GLA_HEADS * GLA_DK
GLA_V_WIDTH = GLA_HEADS * GLA_DV
GLA_GATE_RANK = 16
GLA_TAU = 16.0
GLA_CHUNK = 64

LANES = 128
CODE_PAD = LANES
VMEM_LIMIT = 56 * 1024 * 1024

TM_FFN = 512
TM_PROJ = 512
NA_ROWS_PER_STEP = 16
NA_LOOKAHEAD = 3
GLA_BLOCK = 256


def _rmsnorm(x, g):
    return x * lax.rsqrt(jnp.mean(x * x, axis=-1, keepdims=True) + EPS) * g


def _dot(a, b):
    return jnp.dot(a, b, preferred_element_type=F32)


def _dot_nt(a, b):
    return lax.dot_general(a, b, (((1,), (1,)), ((), ())), preferred_element_type=F32)


def _const_spec(shape):
    return pl.BlockSpec(shape, lambda *_: (0,) * len(shape), pipeline_mode=pl.Buffered(1))


def _params(*sem):
    return pltpu.CompilerParams(dimension_semantics=sem, vmem_limit_bytes=VMEM_LIMIT)


def _ffn_chunks(d_ff):
    chunks, start = [], 0
    while start < d_ff:
        width = min(1024, d_ff - start)
        chunks.append((start, width))
        start += width
    return tuple(chunks)


def _swiglu_update(x, g_ref, wg_ref, wu_ref, wd_ref):
    xb = _rmsnorm(x, g_ref[...]).astype(BF16)
    y = None
    for start, width in _ffn_chunks(wg_ref.shape[1]):
        h = _dot(xb, wg_ref[:, start:start + width])
        u = _dot(xb, wu_ref[:, start:start + width])
        a = (h * jax.nn.sigmoid(h) * u).astype(BF16)
        part = _dot(a, wd_ref[start:start + width, :])
        y = part if y is None else y + part
    return x + 0.5 * y


def _ffn_body(x_ref, g_ref, wg_ref, wu_ref, wd_ref, o_ref):
    o_ref[...] = _swiglu_update(x_ref[...], g_ref, wg_ref, wu_ref, wd_ref)


def _ffn_final_body(x_ref, g_ref, wg_ref, wu_ref, wd_ref, fg_ref, o_ref):
    y = _swiglu_update(x_ref[...], g_ref, wg_ref, wu_ref, wd_ref)
    o_ref[...] = _rmsnorm(y, fg_ref[...])


def _ffn(x, norm_g, wg, wu, wd, final_g=None):
    n, d = x.shape
    d_ff = wg.shape[1]
    tile = pl.BlockSpec((TM_FFN, d), lambda i: (i, 0))
    in_specs = [tile, _const_spec((1, d)), _const_spec((d, d_ff)), _const_spec((d, d_ff)),
                _const_spec((d_ff, d))]
    args = [x, norm_g, wg, wu, wd]
    body = _ffn_body
    if final_g is not None:
        in_specs.append(_const_spec((1, d)))
        args.append(final_g)
        body = _ffn_final_body
    return pl.pallas_call(
        body,
        grid=(n // TM_FFN,),
        in_specs=in_specs,
        out_specs=tile,
        out_shape=jax.ShapeDtypeStruct((n, d), F32),
        compiler_params=_params("parallel"),
        name="ffn_final" if final_g is not None else "ffn",
    )(*args)


def _log_sigmoid(x):
    return jnp.minimum(x, 0.0) - jnp.log1p(jnp.exp(-jnp.abs(x)))


def _inproj_body(x_ref, g_ref, wm_ref, wc_ref, wgate_ref, bgate_ref,
                 naq_ref, nak_ref, nav_ref, gq_ref, gk_ref, gv_ref, gr_ref, lg_ref):
    xb = _rmsnorm(x_ref[...], g_ref[...]).astype(BF16)
    col = 0

    def seg(width):
        nonlocal col
        out = _dot(xb, wm_ref[:, col:col + width])
        col += width
        return out

    naq_ref[...] = (seg(NA_WIDTH) * (NA_HEAD_DIM ** -0.5)).astype(BF16)
    nak_ref[...] = seg(NA_WIDTH).astype(BF16)
    nav_ref[...] = seg(NA_WIDTH).astype(BF16)
    gq_ref[...] = seg(GLA_QK_WIDTH) * (GLA_DK ** -0.5)
    gk_ref[...] = seg(GLA_QK_WIDTH)
    gv_ref[...] = seg(GLA_V_WIDTH).astype(BF16)
    gr_ref[...] = seg(GLA_V_WIDTH)
    codes = _dot(xb, wc_ref[...])
    logits = _dot(codes.astype(BF16), wgate_ref[...].astype(BF16)) + bgate_ref[...]
    lg_ref[...] = _log_sigmoid(logits) * (1.0 / GLA_TAU)


def _inproj(x, norm_g, w_main, w_code, w_gate, b_gate):
    n, d = x.shape
    tm = TM_PROJ

    def tile(width):
        return pl.BlockSpec((tm, width), lambda i: (i, 0))

    widths = (NA_WIDTH, NA_WIDTH, NA_WIDTH, GLA_QK_WIDTH, GLA_QK_WIDTH, GLA_V_WIDTH,
              GLA_V_WIDTH, 2 * GLA_QK_WIDTH)
    dtypes = (BF16, BF16, BF16, F32, F32, BF16, F32, F32)
    return pl.pallas_call(
        _inproj_body,
        grid=(n // tm,),
        in_specs=[tile(d), _const_spec((1, d)), _const_spec(w_main.shape),
                  _const_spec(w_code.shape), _const_spec(w_gate.shape), _const_spec(b_gate.shape)],
        out_specs=[tile(w) for w in widths],
        out_shape=[jax.ShapeDtypeStruct((n, w), dt) for w, dt in zip(widths, dtypes)],
        compiler_params=_params("parallel"),
        name="inproj",
    )(x, norm_g, w_main, w_code, w_gate, b_gate)


_NA_ROW_PAIRS = 2 * WIN_H - 2


def _na_bias_table(rpb):
    qc = np.arange(GRID_W)
    kc = np.arange(GRID_W)
    col_start = np.clip(qc - WIN_W // 2, 0, GRID_W - WIN_W)
    valid = (kc[None, :] >= col_start[:, None]) & (kc[None, :] < col_start[:, None] + WIN_W)
    dc = np.clip(kc[None, :] - qc[:, None] + (WIN_W - 1), 0, 2 * WIN_W - 2)
    onehot = (dc[:, :, None] == np.arange(2 * WIN_W - 1)[None, None, :]).astype(np.float32)
    t = jnp.einsum("hrc,qkc->hrqk", rpb.astype(F32), onehot, precision=lax.Precision.HIGHEST)
    t = jnp.where(valid[None, None], t, NEG_INF)
    return jnp.concatenate([t[:, :-1], t[:, 1:]], axis=-1)


def _na_body(q_ref, k_ref, v_ref, bias_ref, o_ref, s_ref, *, nrows):
    rb = pl.program_id(2)
    half = WIN_H // 2
    lane = lax.broadcasted_iota(jnp.int32, (GRID_W, LANES), 1)
    first_head = lane < NA_HEAD_DIM
    nkeys = WIN_H * GRID_W

    def key_offset(i):
        r = rb * NA_ROWS_PER_STEP + i
        row_start = jnp.clip(r - half, 0, nrows - WIN_H)
        return r, row_start, pl.multiple_of(row_start * GRID_W, GRID_W)

    def scores(i):
        _, _, koff = key_offset(i)
        qr = q_ref[0, i * GRID_W:(i + 1) * GRID_W, :]
        zero = jnp.zeros_like(qr)
        qs = jnp.concatenate([jnp.where(first_head, qr, zero), jnp.where(first_head, zero, qr)], axis=0)
        return _dot_nt(qs, k_ref[0, pl.ds(koff, nkeys), :])

    def finish(i):
        r, row_start, koff = key_offset(i)
        dr0 = row_start - r + (WIN_H - 1)
        bias = jnp.concatenate(
            [jnp.concatenate([bias_ref[hl, dr0 + 2 * wp] for wp in range(WIN_H // 2)], axis=1)
             for hl in range(2)], axis=0)
        s = s_ref[i % (NA_LOOKAHEAD + 1)] + bias
        e = jnp.exp(s - jnp.max(s, axis=-1, keepdims=True))
        o = _dot(e.astype(BF16), v_ref[0, pl.ds(koff, nkeys), :])
        o = o / jnp.sum(e, axis=-1, keepdims=True)
        o_ref[0, i * GRID_W:(i + 1) * GRID_W, :] = jnp.where(first_head, o[:GRID_W], o[GRID_W:])

    for i in range(NA_LOOKAHEAD):
        s_ref[i] = scores(i)
    for i in range(NA_ROWS_PER_STEP):
        ahead = i + NA_LOOKAHEAD
        if ahead < NA_ROWS_PER_STEP:
            s_ref[ahead % (NA_LOOKAHEAD + 1)] = scores(ahead)
        finish(i)


def _na(q, k, v, bias):
    b, s, _ = q.shape
    nrows = s // GRID_W
    qrows = NA_ROWS_PER_STEP * GRID_W
    return pl.pallas_call(
        functools.partial(_na_body, nrows=nrows),
        grid=(NA_HEADS // 2, b, nrows // NA_ROWS_PER_STEP),
        in_specs=[
            pl.BlockSpec((1, qrows, LANES), lambda hp, bi, rb: (bi, rb, hp)),
            pl.BlockSpec((1, s, LANES), lambda hp, bi, rb: (bi, 0, hp)),
            pl.BlockSpec((1, s, LANES), lambda hp, bi, rb: (bi, 0, hp)),
            pl.BlockSpec((2, _NA_ROW_PAIRS, GRID_W, LANES), lambda hp, bi, rb: (hp, 0, 0, 0)),
        ],
        out_specs=pl.BlockSpec((1, qrows, LANES), lambda hp, bi, rb: (bi, rb, hp)),
        out_shape=jax.ShapeDtypeStruct((b, s, NA_WIDTH), F32),
        scratch_shapes=[pltpu.VMEM((NA_LOOKAHEAD + 1, 2 * GRID_W, WIN_H * GRID_W), F32)],
        compiler_params=_params("parallel", "parallel", "arbitrary"),
        name="na",
    )(q, k, v, bias)


def _split3(x):
    x1 = x.astype(BF16)
    r1 = x - x1.astype(F32)
    x2 = r1.astype(BF16)
    x3 = (r1 - x2.astype(F32)).astype(BF16)
    return x1, x2, x3


class _GlaDirection:
    def __init__(self, q_ref, k_ref, v_ref, g_ref, o_ref, s_ref, b_ref, a_ref, u_ref, *,
                 tri, a_mask, last, order):
        self.q_ref, self.k_ref, self.v_ref, self.g_ref = q_ref, k_ref, v_ref, g_ref
        self.o_ref, self.s_ref, self.b_ref, self.a_ref, self.u_ref = o_ref, s_ref, b_ref, a_ref, u_ref
        self.tri, self.a_mask, self.last, self.order = tri, a_mask, last, order

    def cumulative_decay(self):
        self.b_ref[...] = sum(_dot(self.tri, part) for part in _split3(self.g_ref[0]))

    def scores(self, head_masks):
        nchunk = GLA_BLOCK // GLA_CHUNK
        b = self.b_ref[...]
        self.qt = (self.q_ref[0] * jnp.exp(b)).astype(BF16)
        kt = (self.k_ref[0] * jnp.exp(-b)).astype(BF16)
        self.qs = {}
        for c in range(nchunk):
            rows = slice(c * GLA_CHUNK, (c + 1) * GLA_CHUNK)
            qc = self.qt[rows]
            qs = jnp.concatenate([jnp.where(hm, qc, jnp.zeros_like(qc)) for hm in head_masks], axis=0)
            a = _dot_nt(qs, kt[rows])
            self.a_ref[c] = jnp.where(self.a_mask, a, 0.0).astype(BF16)
            self.qs[c] = qs

    def increments(self, chunk_masks):
        nchunk = GLA_BLOCK // GLA_CHUNK
        b = self.b_ref[...]
        b_last = jnp.concatenate(
            [jnp.broadcast_to(b[c * GLA_CHUNK + self.last:c * GLA_CHUNK + self.last + 1, :],
                              (GLA_CHUNK, GLA_QK_WIDTH)) for c in range(nchunk)], axis=0)
        ku_t = (self.k_ref[0] * jnp.exp(b_last - b)).T.astype(BF16)
        self.decay_t = jnp.exp(b_last.T)
        for c in range(nchunk):
            pair = slice((c // 2) * 2 * GLA_CHUNK, (c // 2 + 1) * 2 * GLA_CHUNK)
            ku_pair = ku_t[:, pair]
            ku_c = jnp.where(chunk_masks[c % 2], ku_pair, jnp.zeros_like(ku_pair))
            for h in range(GLA_HEADS):
                hrows = slice(h * GLA_DK, (h + 1) * GLA_DK)
                vcols = slice(h * GLA_DV, (h + 1) * GLA_DV)
                self.u_ref[c, hrows, :] = _dot(ku_c[hrows], self.v_ref[0, pair, vcols])

    def advance_state(self):
        s = self.s_ref[...]
        self.states = {}
        for c in self.order:
            self.states[c] = s.astype(BF16)
            col = c * GLA_CHUNK
            s = self.decay_t[:, col:col + 1] * s + self.u_ref[c]
        self.s_ref[...] = s

    def output_chunk(self, c):
        rows = slice(c * GLA_CHUNK, (c + 1) * GLA_CHUNK)
        inter = _dot(self.qs[c], self.states[c])
        for h in range(GLA_HEADS):
            arows = slice(h * GLA_CHUNK, (h + 1) * GLA_CHUNK)
            vcols = slice(h * GLA_DV, (h + 1) * GLA_DV)
            intra = _dot(self.a_ref[c, arows, :], self.v_ref[0, rows, vcols])
            self.o_ref[0, rows, vcols] = inter[arows] + intra


def _gla_body(qf_ref, kf_ref, vf_ref, gf_ref, qb_ref, kb_ref, vb_ref, gb_ref,
              of_ref, ob_ref, sf_ref, sb_ref, bf_ref, bb_ref, af_ref, ab_ref, uf_ref, ub_ref):
    @pl.when(pl.program_id(1) == 0)
    def _():
        sf_ref[...] = jnp.zeros_like(sf_ref)
        sb_ref[...] = jnp.zeros_like(sb_ref)

    shift = GLA_CHUNK.bit_length() - 1
    row = lax.broadcasted_iota(jnp.int32, (GLA_BLOCK, GLA_BLOCK), 0)
    col = lax.broadcasted_iota(jnp.int32, (GLA_BLOCK, GLA_BLOCK), 1)
    same_chunk = (row >> shift) == (col >> shift)
    lower = jnp.where(same_chunk & (row >= col), 1.0, 0.0).astype(BF16)
    upper = jnp.where(same_chunk & (row <= col), 1.0, 0.0).astype(BF16)
    nchunk = GLA_BLOCK // GLA_CHUNK
    lane = lax.broadcasted_iota(jnp.int32, (GLA_CHUNK, GLA_QK_WIDTH), 1)
    head_masks = [(lane >> shift) == h for h in range(GLA_HEADS)]
    pair_lane = lax.broadcasted_iota(jnp.int32, (GLA_QK_WIDTH, 2 * GLA_CHUNK), 1)
    chunk_masks = [(pair_lane >> shift) == j for j in range(2)]
    t = lax.broadcasted_iota(jnp.int32, (GLA_HEADS * GLA_CHUNK, GLA_CHUNK), 0) & (GLA_CHUNK - 1)
    s = lax.broadcasted_iota(jnp.int32, (GLA_HEADS * GLA_CHUNK, GLA_CHUNK), 1)
    fwd = _GlaDirection(qf_ref, kf_ref, vf_ref, gf_ref, of_ref, sf_ref, bf_ref, af_ref, uf_ref,
                        tri=lower, a_mask=t >= s, last=GLA_CHUNK - 1, order=tuple(range(nchunk)))
    bwd = _GlaDirection(qb_ref, kb_ref, vb_ref, gb_ref, ob_ref, sb_ref, bb_ref, ab_ref, ub_ref,
                        tri=upper, a_mask=t < s, last=0, order=tuple(reversed(range(nchunk))))
    fwd.cumulative_decay()
    bwd.cumulative_decay()
    fwd.scores(head_masks)
    bwd.scores(head_masks)
    fwd.increments(chunk_masks)
    bwd.increments(chunk_masks)
    fwd.advance_state()
    bwd.advance_state()
    for cf, cb in zip(fwd.order, bwd.order):
        fwd.output_chunk(cf)
        bwd.output_chunk(cb)


def _gla(q, k, v, lg):
    b, s, _ = q.shape
    nb = s // GLA_BLOCK

    def fwd(width, j=0):
        return pl.BlockSpec((1, GLA_BLOCK, width), lambda bi, n: (bi, n, j))

    def bwd(width, j=0):
        return pl.BlockSpec((1, GLA_BLOCK, width), lambda bi, n: (bi, nb - 1 - n, j))

    state = pltpu.VMEM((GLA_QK_WIDTH, GLA_DV), F32)
    nchunk = GLA_BLOCK // GLA_CHUNK
    decay = pltpu.VMEM((GLA_BLOCK, GLA_QK_WIDTH), F32)
    weights = pltpu.VMEM((nchunk, GLA_HEADS * GLA_CHUNK, GLA_CHUNK), BF16)
    increments = pltpu.VMEM((nchunk, GLA_QK_WIDTH, GLA_DV), F32)
    return pl.pallas_call(
        _gla_body,
        grid=(b, nb),
        in_specs=[fwd(GLA_QK_WIDTH), fwd(GLA_QK_WIDTH), fwd(GLA_V_WIDTH), fwd(GLA_QK_WIDTH, 0),
                  bwd(GLA_QK_WIDTH), bwd(GLA_QK_WIDTH), bwd(GLA_V_WIDTH), bwd(GLA_QK_WIDTH, 1)],
        out_specs=[fwd(GLA_V_WIDTH), bwd(GLA_V_WIDTH)],
        out_shape=[jax.ShapeDtypeStruct((b, s, GLA_V_WIDTH), F32)] * 2,
        scratch_shapes=[state, state, decay, decay, weights, weights, increments, increments],
        compiler_params=_params("parallel", "arbitrary"),
        name="gla",
    )(q, k, v, lg, q, k, v, lg)


def _outproj_body(x_ref, na_ref, nag_ref, of_ref, ob_ref, gg_ref, gr_ref, wna_ref, wgla_ref, o_ref):
    na = _rmsnorm(na_ref[...], nag_ref[...]).astype(BF16)
    y = _dot(na, wna_ref[...])
    o = of_ref[...] + ob_ref[...]
    r = gr_ref[...]
    gate = r * jax.nn.sigmoid(r)
    heads = []
    for h in range(GLA_HEADS):
        cols = slice(h * GLA_DV, (h + 1) * GLA_DV)
        heads.append((_rmsnorm(o[:, cols], gg_ref[...]) * gate[:, cols]).astype(BF16))
    y = y + _dot(jnp.concatenate(heads, axis=-1), wgla_ref[...])
    o_ref[...] = x_ref[...] + y


def _outproj(x, na_o, na_gain, o_f, o_b, gla_gain, g_r, w_na, w_gla):
    n, d = x.shape
    tm = TM_PROJ

    def tile(width):
        return pl.BlockSpec((tm, width), lambda i: (i, 0))

    return pl.pallas_call(
        _outproj_body,
        grid=(n // tm,),
        in_specs=[tile(d), tile(NA_WIDTH), _const_spec((1, NA_WIDTH)), tile(GLA_V_WIDTH),
                  tile(GLA_V_WIDTH), _const_spec((1, GLA_DV)), tile(GLA_V_WIDTH),
                  _const_spec(w_na.shape), _const_spec(w_gla.shape)],
        out_specs=tile(d),
        out_shape=jax.ShapeDtypeStruct((n, d), F32),
        compiler_params=_params("parallel"),
        name="outproj",
    )(x, na_o, na_gain, o_f, o_b, gla_gain, g_r, w_na, w_gla)


def _gate_weights(w_f, b_f, w_b, b_b):
    w = jnp.zeros((CODE_PAD, 2 * GLA_QK_WIDTH), F32)
    w = w.at[:GLA_GATE_RANK, :GLA_QK_WIDTH].set(w_f)
    w = w.at[GLA_GATE_RANK:2 * GLA_GATE_RANK, GLA_QK_WIDTH:].set(w_b)
    return w, jnp.concatenate([b_f, b_b])[None, :]


def kernel(x, ffn1_norm, ffn1_wg, ffn1_wu, ffn1_wd, mix_norm, w_in, na_rpb, na_gain, w_gate_f, b_gate_f, w_gate_b, b_gate_b, gla_gain, w_out, ffn2_norm, ffn2_wg, ffn2_wu, ffn2_wd, final_norm):
    bsz, seq, d = x.shape
    depth = ffn1_norm.shape[0]
    n = bsz * seq
    main_width = 3 * NA_WIDTH + 2 * GLA_QK_WIDTH + 2 * GLA_V_WIDTH
    xt = x.reshape(n, d)
    for l in range(depth):
        xt = _ffn(xt, ffn1_norm[l][None], ffn1_wg[l].astype(BF16), ffn1_wu[l].astype(BF16),
                  ffn1_wd[l].astype(BF16))
        w_main = w_in[l][:, :main_width].astype(BF16)
        w_code = jnp.pad(w_in[l][:, main_width:], ((0, 0), (0, CODE_PAD - 2 * GLA_GATE_RANK))).astype(BF16)
        w_gate, b_gate = _gate_weights(w_gate_f[l], b_gate_f[l], w_gate_b[l], b_gate_b[l])
        na_q, na_k, na_v, g_q, g_k, g_v, g_r, lg = _inproj(xt, mix_norm[l][None], w_main, w_code,
                                                           w_gate, b_gate)
        to3 = lambda a: a.reshape(bsz, seq, a.shape[-1])
        na_o = _na(to3(na_q), to3(na_k), to3(na_v), _na_bias_table(na_rpb[l]))
        o_f, o_b = _gla(to3(g_q), to3(g_k), to3(g_v), to3(lg))
        w_o = w_out[l].astype(BF16)
        xt = _outproj(xt, na_o.reshape(n, NA_WIDTH), na_gain[l][None], o_f.reshape(n, GLA_V_WIDTH),
                      o_b.reshape(n, GLA_V_WIDTH), gla_gain[l][None], g_r, w_o[:NA_WIDTH], w_o[NA_WIDTH:])
        xt = _ffn(xt, ffn2_norm[l][None], ffn2_wg[l].astype(BF16), ffn2_wu[l].astype(BF16),
                  ffn2_wd[l].astype(BF16), final_g=final_norm[None] if l == depth - 1 else None)
    return xt.reshape(bsz, seq, d)
```

```python
import functools

import numpy as np
import jax
import jax.numpy as jnp
from jax import lax
from jax.experimental import pallas as pl
from jax.experimental.pallas import tpu as pltpu

F32 = jnp.float32
BF16 = jnp.bfloat16

EPS = 1e-6
NEG_INF = -1e30

GRID_W = 64
WIN_H = 8
WIN_W = 16
NA_HEADS = 8
NA_HEAD_DIM = 64
NA_WIDTH = NA_HEADS * NA_HEAD_DIM
GLA_HEADS = 4
GLA_DK = 64
GLA_DV = 128
GLA_QK_WIDTH = GLA_HEADS * GLA_DK
GLA_V_WIDTH = GLA_HEADS * GLA_DV
GLA_GATE_RANK = 16
GLA_TAU = 16.0
GLA_CHUNK = 64

LANES = 128
CODE_PAD = LANES
VMEM_LIMIT = 56 * 1024 * 1024

TM_FFN = 1024
FFN_SUBTILE = 512
TM_PROJ = 512
NA_ROWS_PER_STEP = 32
NA_LOOKAHEAD = 3
GLA_BLOCK = 512
GLA_SUB = 256


def _rmsnorm(x, g):
    return x * lax.rsqrt(jnp.mean(x * x, axis=-1, keepdims=True) + EPS) * g


def _dot(a, b):
    return jnp.dot(a, b, preferred_element_type=F32)


def _dot_nt(a, b):
    return lax.dot_general(a, b, (((1,), (1,)), ((), ())), preferred_element_type=F32)


def _const_spec(shape):
    return pl.BlockSpec(shape, lambda *_: (0,) * len(shape), pipeline_mode=pl.Buffered(1))


def _params(*sem):
    return pltpu.CompilerParams(dimension_semantics=sem, vmem_limit_bytes=VMEM_LIMIT)


def _ffn_chunks(d_ff):
    chunks, start = [], 0
    while start < d_ff:
        width = min(1024, d_ff - start)
        chunks.append((start, width))
        start += width
    return tuple(chunks)


def _swiglu_update(x, g_ref, wg_ref, wu_ref, wd_ref):
    xb = _rmsnorm(x, g_ref[...]).astype(BF16)
    y = None
    for start, width in _ffn_chunks(wg_ref.shape[1]):
        h = _dot(xb, wg_ref[:, start:start + width])
        u = _dot(xb, wu_ref[:, start:start + width])
        a = (h * jax.nn.sigmoid(h) * u).astype(BF16)
        part = _dot(a, wd_ref[start:start + width, :])
        y = part if y is None else y + part
    return x + 0.5 * y


def _ffn_body(x_ref, g_ref, wg_ref, wu_ref, wd_ref, o_ref):
    for r in range(0, TM_FFN, FFN_SUBTILE):
        rows = slice(r, r + FFN_SUBTILE)
        o_ref[rows, :] = _swiglu_update(x_ref[rows, :], g_ref, wg_ref, wu_ref, wd_ref)


def _ffn(x, norm_g, wg, wu, wd):
    n, d = x.shape
    d_ff = wg.shape[1]
    tile = pl.BlockSpec((TM_FFN, d), lambda i: (i, 0))
    return pl.pallas_call(
        _ffn_body,
        grid=(n // TM_FFN,),
        in_specs=[tile, _const_spec((1, d)), _const_spec((d, d_ff)), _const_spec((d, d_ff)),
                  _const_spec((d_ff, d))],
        out_specs=tile,
        out_shape=jax.ShapeDtypeStruct((n, d), F32),
        compiler_params=_params("parallel"),
        name="ffn",
    )(x, norm_g, wg, wu, wd)


def _log_sigmoid(x):
    return jnp.minimum(x, 0.0) - jnp.log1p(jnp.exp(-jnp.abs(x)))


def _inproj_body(x_ref, g_ref, wm_ref, wc_ref, wgate_ref, bgate_ref,
                 naq_ref, nak_ref, nav_ref, gq_ref, gk_ref, gv_ref, gr_ref, lg_ref):
    xb = _rmsnorm(x_ref[...], g_ref[...]).astype(BF16)
    col = 0

    def seg(width):
        nonlocal col
        out = _dot(xb, wm_ref[:, col:col + width])
        col += width
        return out

    naq_ref[...] = (seg(NA_WIDTH) * (NA_HEAD_DIM ** -0.5)).astype(BF16)
    nak_ref[...] = seg(NA_WIDTH).astype(BF16)
    nav_ref[...] = seg(NA_WIDTH).astype(BF16)
    gq_ref[...] = seg(GLA_QK_WIDTH) * (GLA_DK ** -0.5)
    gk_ref[...] = seg(GLA_QK_WIDTH)
    gv_ref[...] = seg(GLA_V_WIDTH).astype(BF16)
    gr_ref[...] = seg(GLA_V_WIDTH)
    codes = _dot(xb, wc_ref[...])
    logits = _dot(codes.astype(BF16), wgate_ref[...].astype(BF16)) + bgate_ref[...]
    lg_ref[...] = _log_sigmoid(logits) * (1.0 / GLA_TAU)


def _inproj(x, norm_g, w_main, w_code, w_gate, b_gate):
    n, d = x.shape
    tm = TM_PROJ

    def tile(width):
        return pl.BlockSpec((tm, width), lambda i: (i, 0))

    widths = (NA_WIDTH, NA_WIDTH, NA_WIDTH, GLA_QK_WIDTH, GLA_QK_WIDTH, GLA_V_WIDTH,
              GLA_V_WIDTH, 2 * GLA_QK_WIDTH)
    dtypes = (BF16, BF16, BF16, F32, F32, BF16, F32, F32)
    return pl.pallas_call(
        _inproj_body,
        grid=(n // tm,),
        in_specs=[tile(d), _const_spec((1, d)), _const_spec(w_main.shape),
                  _const_spec(w_code.shape), _const_spec(w_gate.shape), _const_spec(b_gate.shape)],
        out_specs=[tile(w) for w in widths],
        out_shape=[jax.ShapeDtypeStruct((n, w), dt) for w, dt in zip(widths, dtypes)],
        compiler_params=_params("parallel"),
        name="inproj",
    )(x, norm_g, w_main, w_code, w_gate, b_gate)


_NA_ROW_PAIRS = 2 * WIN_H - 2


def _na_bias_table(rpb):
    qc = np.arange(GRID_W)
    kc = np.arange(GRID_W)
    col_start = np.clip(qc - WIN_W // 2, 0, GRID_W - WIN_W)
    valid = (kc[None, :] >= col_start[:, None]) & (kc[None, :] < col_start[:, None] + WIN_W)
    dc = np.clip(kc[None, :] - qc[:, None] + (WIN_W - 1), 0, 2 * WIN_W - 2)
    onehot = (dc[:, :, None] == np.arange(2 * WIN_W - 1)[None, None, :]).astype(np.float32)
    t = jnp.einsum("hrc,qkc->hrqk", rpb.astype(F32), onehot, precision=lax.Precision.HIGHEST)
    t = jnp.where(valid[None, None], t, NEG_INF)
    return jnp.concatenate([t[:, :-1], t[:, 1:]], axis=-1)


def _na_body(q_ref, k_ref, v_ref, bias_ref, o_ref, s_ref, *, nrows):
    rb = pl.program_id(2)
    half = WIN_H // 2
    lane = lax.broadcasted_iota(jnp.int32, (GRID_W, LANES), 1)
    first_head = lane < NA_HEAD_DIM
    nkeys = WIN_H * GRID_W

    def key_offset(i):
        r = rb * NA_ROWS_PER_STEP + i
        row_start = jnp.clip(r - half, 0, nrows - WIN_H)
        return r, row_start, pl.multiple_of(row_start * GRID_W, GRID_W)

    def scores(i):
        _, _, koff = key_offset(i)
        qr = q_ref[0, i * GRID_W:(i + 1) * GRID_W, :]
        zero = jnp.zeros_like(qr)
        qs = jnp.concatenate([jnp.where(first_head, qr, zero), jnp.where(first_head, zero, qr)], axis=0)
        return _dot_nt(qs, k_ref[0, pl.ds(koff, nkeys), :])

    def finish(i):
        r, row_start, koff = key_offset(i)
        dr0 = row_start - r + (WIN_H - 1)
        bias = jnp.concatenate(
            [jnp.concatenate([bias_ref[hl, dr0 + 2 * wp] for wp in range(WIN_H // 2)], axis=1)
             for hl in range(2)], axis=0)
        s = s_ref[i % (NA_LOOKAHEAD + 1)] + bias
        e = jnp.exp(s - jnp.max(s, axis=-1, keepdims=True))
        o = _dot(e.astype(BF16), v_ref[0, pl.ds(koff, nkeys), :])
        o = o / jnp.sum(e, axis=-1, keepdims=True)
        o_ref[0, i * GRID_W:(i + 1) * GRID_W, :] = jnp.where(first_head, o[:GRID_W], o[GRID_W:])

    for i in range(NA_LOOKAHEAD):
        s_ref[i] = scores(i)
    for i in range(NA_ROWS_PER_STEP):
        ahead = i + NA_LOOKAHEAD
        if ahead < NA_ROWS_PER_STEP:
            s_ref[ahead % (NA_LOOKAHEAD + 1)] = scores(ahead)
        finish(i)


def _na(q, k, v, bias):
    b, s, _ = q.shape
    nrows = s // GRID_W
    qrows = NA_ROWS_PER_STEP * GRID_W
    return pl.pallas_call(
        functools.partial(_na_body, nrows=nrows),
        grid=(NA_HEADS // 2, b, nrows // NA_ROWS_PER_STEP),
        in_specs=[
            pl.BlockSpec((1, qrows, LANES), lambda hp, bi, rb: (bi, rb, hp)),
            pl.BlockSpec((1, s, LANES), lambda hp, bi, rb: (bi, 0, hp)),
            pl.BlockSpec((1, s, LANES), lambda hp, bi, rb: (bi, 0, hp)),
            pl.BlockSpec((2, _NA_ROW_PAIRS, GRID_W, LANES), lambda hp, bi, rb: (hp, 0, 0, 0)),
        ],
        out_specs=pl.BlockSpec((1, qrows, LANES), lambda hp, bi, rb: (bi, rb, hp)),
        out_shape=jax.ShapeDtypeStruct((b, s, NA_WIDTH), F32),
        scratch_shapes=[pltpu.VMEM((NA_LOOKAHEAD + 1, 2 * GRID_W, WIN_H * GRID_W), F32)],
        compiler_params=_params("parallel", "parallel", "arbitrary"),
        name="na",
    )(q, k, v, bias)


def _split3(x):
    x1 = x.astype(BF16)
    r1 = x - x1.astype(F32)
    x2 = r1.astype(BF16)
    x3 = (r1 - x2.astype(F32)).astype(BF16)
    return x1, x2, x3


class _GlaSubBlock:
    def __init__(self, q_ref, k_ref, v_ref, g_ref, o_ref, s_ref, b_ref, a_ref, u_ref, *,
                 r0, slot, tri, a_mask, last, order):
        self.q_ref, self.k_ref, self.v_ref, self.g_ref = q_ref, k_ref, v_ref, g_ref
        self.o_ref, self.s_ref, self.b_ref, self.a_ref, self.u_ref = o_ref, s_ref, b_ref, a_ref, u_ref
        self.r0, self.slot = r0, slot
        self.rows = slice(r0, r0 + GLA_SUB)
        self.tri, self.a_mask, self.last, self.order = tri, a_mask, last, order

    def cumulative_decay(self):
        g = self.g_ref[0, self.rows, :]
        self.b_ref[self.slot] = sum(_dot(self.tri, part) for part in _split3(g))

    def scores(self, head_masks):
        b = self.b_ref[self.slot]
        self.qt = (self.q_ref[0, self.rows, :] * jnp.exp(b)).astype(BF16)
        kt = (self.k_ref[0, self.rows, :] * jnp.exp(-b)).astype(BF16)
        self.qs = {}
        for c in range(GLA_SUB // GLA_CHUNK):
            rows = slice(c * GLA_CHUNK, (c + 1) * GLA_CHUNK)
            qc = self.qt[rows]
            qs = jnp.concatenate([jnp.where(hm, qc, jnp.zeros_like(qc)) for hm in head_masks], axis=0)
            a = _dot_nt(qs, kt[rows])
            self.a_ref[self.slot, c] = jnp.where(self.a_mask, a, 0.0).astype(BF16)
            self.qs[c] = qs

    def increments(self, chunk_masks):
        nchunk = GLA_SUB // GLA_CHUNK
        b = self.b_ref[self.slot]
        b_last = jnp.concatenate(
            [jnp.broadcast_to(b[c * GLA_CHUNK + self.last:c * GLA_CHUNK + self.last + 1, :],
                              (GLA_CHUNK, GLA_QK_WIDTH)) for c in range(nchunk)], axis=0)
        ku_t = (self.k_ref[0, self.rows, :] * jnp.exp(b_last - b)).T.astype(BF16)
        self.decay_t = jnp.exp(b_last.T)
        for c in range(nchunk):
            pair = slice((c // 2) * 2 * GLA_CHUNK, (c // 2 + 1) * 2 * GLA_CHUNK)
            vrows = slice(self.r0 + pair.start, self.r0 + pair.stop)
            ku_pair = ku_t[:, pair]
            ku_c = jnp.where(chunk_masks[c % 2], ku_pair, jnp.zeros_like(ku_pair))
            for h in range(GLA_HEADS):
                hrows = slice(h * GLA_DK, (h + 1) * GLA_DK)
                vcols = slice(h * GLA_DV, (h + 1) * GLA_DV)
                self.u_ref[self.slot, c, hrows, :] = _dot(ku_c[hrows], self.v_ref[0, vrows, vcols])

    def advance_state(self):
        s = self.s_ref[...]
        self.states = {}
        for c in self.order:
            self.states[c] = s.astype(BF16)
            col = c * GLA_CHUNK
            s = self.decay_t[:, col:col + 1] * s + self.u_ref[self.slot, c]
        self.s_ref[...] = s

    def output_chunk(self, c):
        rows = slice(self.r0 + c * GLA_CHUNK, self.r0 + (c + 1) * GLA_CHUNK)
        inter = _dot(self.qs[c], self.states[c])
        for h in range(GLA_HEADS):
            arows = slice(h * GLA_CHUNK, (h + 1) * GLA_CHUNK)
            vcols = slice(h * GLA_DV, (h + 1) * GLA_DV)
            intra = _dot(self.a_ref[self.slot, c, arows, :], self.v_ref[0, rows, vcols])
            self.o_ref[0, rows, vcols] = inter[arows] + intra


def _gla_body(qf_ref, kf_ref, vf_ref, gf_ref, qb_ref, kb_ref, vb_ref, gb_ref,
              of_ref, ob_ref, sf_ref, sb_ref, bf_ref, bb_ref, af_ref, ab_ref, uf_ref, ub_ref):
    @pl.when(pl.program_id(1) == 0)
    def _():
        sf_ref[...] = jnp.zeros_like(sf_ref)
        sb_ref[...] = jnp.zeros_like(sb_ref)

    shift = GLA_CHUNK.bit_length() - 1
    row = lax.broadcasted_iota(jnp.int32, (GLA_SUB, GLA_SUB), 0)
    col = lax.broadcasted_iota(jnp.int32, (GLA_SUB, GLA_SUB), 1)
    same_chunk = (row >> shift) == (col >> shift)
    lower = jnp.where(same_chunk & (row >= col), 1.0, 0.0).astype(BF16)
    upper = jnp.where(same_chunk & (row <= col), 1.0, 0.0).astype(BF16)
    nchunk = GLA_SUB // GLA_CHUNK
    nsub = GLA_BLOCK // GLA_SUB
    lane = lax.broadcasted_iota(jnp.int32, (GLA_CHUNK, GLA_QK_WIDTH), 1)
    head_masks = [(lane >> shift) == h for h in range(GLA_HEADS)]
    pair_lane = lax.broadcasted_iota(jnp.int32, (GLA_QK_WIDTH, 2 * GLA_CHUNK), 1)
    chunk_masks = [(pair_lane >> shift) == j for j in range(2)]
    t = lax.broadcasted_iota(jnp.int32, (GLA_HEADS * GLA_CHUNK, GLA_CHUNK), 0) & (GLA_CHUNK - 1)
    s = lax.broadcasted_iota(jnp.int32, (GLA_HEADS * GLA_CHUNK, GLA_CHUNK), 1)
    fwd = [_GlaSubBlock(qf_ref, kf_ref, vf_ref, gf_ref, of_ref, sf_ref, bf_ref, af_ref, uf_ref,
                        r0=i * GLA_SUB, slot=i, tri=lower, a_mask=t >= s, last=GLA_CHUNK - 1,
                        order=tuple(range(nchunk))) for i in range(nsub)]
    bwd = [_GlaSubBlock(qb_ref, kb_ref, vb_ref, gb_ref, ob_ref, sb_ref, bb_ref, ab_ref, ub_ref,
                        r0=i * GLA_SUB, slot=i, tri=upper, a_mask=t < s, last=0,
                        order=tuple(reversed(range(nchunk)))) for i in reversed(range(nsub))]
    both = [blk for pair in zip(fwd, bwd) for blk in pair]
    for blk in both:
        blk.cumulative_decay()
    for blk in both:
        blk.scores(head_masks)
    for blk in both:
        blk.increments(chunk_masks)
    for blk in both:
        blk.advance_state()
    for f, b in zip(fwd, bwd):
        for cf, cb in zip(f.order, b.order):
            f.output_chunk(cf)
            b.output_chunk(cb)


def _gla(q, k, v, lg):
    b, s, _ = q.shape
    nb = s // GLA_BLOCK

    def fwd(width, j=0):
        return pl.BlockSpec((1, GLA_BLOCK, width), lambda bi, n: (bi, n, j))

    def bwd(width, j=0):
        return pl.BlockSpec((1, GLA_BLOCK, width), lambda bi, n: (bi, nb - 1 - n, j))

    nsub = GLA_BLOCK // GLA_SUB
    nchunk = GLA_SUB // GLA_CHUNK
    state = pltpu.VMEM((GLA_QK_WIDTH, GLA_DV), F32)
    decay = pltpu.VMEM((nsub, GLA_SUB, GLA_QK_WIDTH), F32)
    weights = pltpu.VMEM((nsub, nchunk, GLA_HEADS * GLA_CHUNK, GLA_CHUNK), BF16)
    increments = pltpu.VMEM((nsub, nchunk, GLA_QK_WIDTH, GLA_DV), F32)
    return pl.pallas_call(
        _gla_body,
        grid=(b, nb),
        in_specs=[fwd(GLA_QK_WIDTH), fwd(GLA_QK_WIDTH), fwd(GLA_V_WIDTH), fwd(GLA_QK_WIDTH, 0),
                  bwd(GLA_QK_WIDTH), bwd(GLA_QK_WIDTH), bwd(GLA_V_WIDTH), bwd(GLA_QK_WIDTH, 1)],
        out_specs=[fwd(GLA_V_WIDTH), bwd(GLA_V_WIDTH)],
        out_shape=[jax.ShapeDtypeStruct((b, s, GLA_V_WIDTH), F32)] * 2,
        scratch_shapes=[state, state, decay, decay, weights, weights, increments, increments],
        compiler_params=_params("parallel", "arbitrary"),
        name="gla",
    )(q, k, v, lg, q, k, v, lg)


def _mix_update(x, na_ref, nag_ref, of_ref, ob_ref, gg_ref, gr_ref, wna_ref, wgla_ref):
    na = _rmsnorm(na_ref[...], nag_ref[...]).astype(BF16)
    y = _dot(na, wna_ref[...])
    o = of_ref[...] + ob_ref[...]
    r = gr_ref[...]
    gate = r * jax.nn.sigmoid(r)
    heads = []
    for h in range(GLA_HEADS):
        cols = slice(h * GLA_DV, (h + 1) * GLA_DV)
        heads.append((_rmsnorm(o[:, cols], gg_ref[...]) * gate[:, cols]).astype(BF16))
    return x + y + _dot(jnp.concatenate(heads, axis=-1), wgla_ref[...])


def _mix_ffn_body(x_ref, na_ref, nag_ref, of_ref, ob_ref, gg_ref, gr_ref, wna_ref, wgla_ref,
                  g_ref, wg_ref, wu_ref, wd_ref, o_ref):
    x = _mix_update(x_ref[...], na_ref, nag_ref, of_ref, ob_ref, gg_ref, gr_ref, wna_ref, wgla_ref)
    o_ref[...] = _swiglu_update(x, g_ref, wg_ref, wu_ref, wd_ref)


def _mix_ffn_final_body(x_ref, na_ref, nag_ref, of_ref, ob_ref, gg_ref, gr_ref, wna_ref, wgla_ref,
                        g_ref, wg_ref, wu_ref, wd_ref, fg_ref, o_ref):
    x = _mix_update(x_ref[...], na_ref, nag_ref, of_ref, ob_ref, gg_ref, gr_ref, wna_ref, wgla_ref)
    o_ref[...] = _rmsnorm(_swiglu_update(x, g_ref, wg_ref, wu_ref, wd_ref), fg_ref[...])


def _mix_ffn(x, na_o, na_gain, o_f, o_b, gla_gain, g_r, w_na, w_gla, norm_g, wg, wu, wd,
             final_g=None):
    n, d = x.shape
    d_ff = wg.shape[1]
    tm = TM_PROJ

    def tile(width):
        return pl.BlockSpec((tm, width), lambda i: (i, 0))

    in_specs = [tile(d), tile(NA_WIDTH), _const_spec((1, NA_WIDTH)), tile(GLA_V_WIDTH),
                tile(GLA_V_WIDTH), _const_spec((1, GLA_DV)), tile(GLA_V_WIDTH),
                _const_spec(w_na.shape), _const_spec(w_gla.shape),
                _const_spec((1, d)), _const_spec((d, d_ff)), _const_spec((d, d_ff)),
                _const_spec((d_ff, d))]
    args = [x, na_o, na_gain, o_f, o_b, gla_gain, g_r, w_na, w_gla, norm_g, wg, wu, wd]
    body = _mix_ffn_body
    if final_g is not None:
        in_specs.append(_const_spec((1, d)))
        args.append(final_g)
        body = _mix_ffn_final_body
    return pl.pallas_call(
        body,
        grid=(n // tm,),
        in_specs=in_specs,
        out_specs=tile(d),
        out_shape=jax.ShapeDtypeStruct((n, d), F32),
        compiler_params=_params("parallel"),
        name="mix_ffn_final" if final_g is not None else "mix_ffn",
    )(*args)


def _gate_weights(w_f, b_f, w_b, b_b):
    w = jnp.zeros((CODE_PAD, 2 * GLA_QK_WIDTH), F32)
    w = w.at[:GLA_GATE_RANK, :GLA_QK_WIDTH].set(w_f)
    w = w.at[GLA_GATE_RANK:2 * GLA_GATE_RANK, GLA_QK_WIDTH:].set(w_b)
    return w, jnp.concatenate([b_f, b_b])[None, :]


def kernel(x, ffn1_norm, ffn1_wg, ffn1_wu, ffn1_wd, mix_norm, w_in, na_rpb, na_gain, w_gate_f, b_gate_f, w_gate_b, b_gate_b, gla_gain, w_out, ffn2_norm, ffn2_wg, ffn2_wu, ffn2_wd, final_norm):
    bsz, seq, d = x.shape
    depth = ffn1_norm.shape[0]
    n = bsz * seq
    main_width = 3 * NA_WIDTH + 2 * GLA_QK_WIDTH + 2 * GLA_V_WIDTH
    xt = x.reshape(n, d)
    for l in range(depth):
        xt = _ffn(xt, ffn1_norm[l][None], ffn1_wg[l].astype(BF16), ffn1_wu[l].astype(BF16),
                  ffn1_wd[l].astype(BF16))
        w_main = w_in[l][:, :main_width].astype(BF16)
        w_code = jnp.pad(w_in[l][:, main_width:], ((0, 0), (0, CODE_PAD - 2 * GLA_GATE_RANK))).astype(BF16)
        w_gate, b_gate = _gate_weights(w_gate_f[l], b_gate_f[l], w_gate_b[l], b_gate_b[l])
        na_q, na_k, na_v, g_q, g_k, g_v, g_r, lg = _inproj(xt, mix_norm[l][None], w_main, w_code,
                                                           w_gate, b_gate)
        to3 = lambda a: a.reshape(bsz, seq, a.shape[-1])
        na_o = _na(to3(na_q), to3(na_k), to3(na_v), _na_bias_table(na_rpb[l]))
        o_f, o_b = _gla(to3(g_q), to3(g_k), to3(g_v), to3(lg))
        w_o = w_out[l].astype(BF16)
        xt = _mix_ffn(xt, na_o.reshape(n, NA_WIDTH), na_gain[l][None], o_f.reshape(n, GLA_V_WIDTH),
                      o_b.reshape(n, GLA_V_WIDTH), gla_gain[l][None], g_r, w_o[:NA_WIDTH], w_o[NA_WIDTH:],
                      ffn2_norm[l][None], ffn2_wg[l].astype(BF16), ffn2_wu[l].astype(BF16),
                      ffn2_wd[l].astype(BF16), final_g=final_norm[None] if l == depth - 1 else None)
    return xt.reshape(bsz, seq, d)
```

```python
import functools

import numpy as np
import jax
import jax.numpy as jnp
from jax import lax
from jax.experimental import pallas as pl
from jax.experimental.pallas import tpu as pltpu

F32 = jnp.float32
BF16 = jnp.bfloat16

EPS = 1e-6
NEG_INF = -1e30

GRID_W = 64
WIN_H = 8
WIN_W = 16
NA_HEADS = 8
NA_HEAD_DIM = 64
NA_WIDTH = NA_HEADS * NA_HEAD_DIM
GLA_HEADS = 4
GLA_DK = 64
GLA_DV = 128
GLA_QK_WIDTH = GLA_HEADS * GLA_DK
GLA_V_WIDTH = GLA_HEADS * GLA_DV
GLA_GATE_RANK = 16
GLA_TAU = 16.0
GLA_CHUNK = 64
MAIN_WIDTH = 3 * NA_WIDTH + 2 * GLA_QK_WIDTH + 2 * GLA_V_WIDTH

LANES = 128
CODE_PAD = LANES
VMEM_LIMIT = 56 * 1024 * 1024

TM_FFN = 1024
FFN_SUBTILE = 512
TM_PROJ = 512
PROJ_SUBTILE = 256
PROJ_DOT_WIDTH = 1024
CAST_CHUNK = 256
NA_ROWS_PER_STEP = 32
NA_LOOKAHEAD = 3
GLA_BLOCK = 512
GLA_SUB = 256


def _rmsnorm(x, g):
    return x * lax.rsqrt(jnp.mean(x * x, axis=-1, keepdims=True) + EPS) * g


def _dot(a, b):
    return jnp.dot(a, b, preferred_element_type=F32)


def _dot_nt(a, b):
    return lax.dot_general(a, b, (((1,), (1,)), ((), ())), preferred_element_type=F32)


def _const_spec(shape):
    return pl.BlockSpec(shape, lambda *_: (0,) * len(shape), pipeline_mode=pl.Buffered(1))


def _params(*sem):
    return pltpu.CompilerParams(dimension_semantics=sem, vmem_limit_bytes=VMEM_LIMIT)


def _ffn_chunks(d_ff):
    chunks, start = [], 0
    while start < d_ff:
        width = min(1024, d_ff - start)
        chunks.append((start, width))
        start += width
    return tuple(chunks)


def _swiglu_update(x, g_ref, wg_ref, wu_ref, wd_ref):
    xb = _rmsnorm(x, g_ref[...]).astype(BF16)
    y = None
    for start, width in _ffn_chunks(wg_ref.shape[1]):
        h = _dot(xb, wg_ref[:, start:start + width])
        u = _dot(xb, wu_ref[:, start:start + width])
        a = (h * jax.nn.sigmoid(h) * u).astype(BF16)
        part = _dot(a, wd_ref[start:start + width, :])
        y = part if y is None else y + part
    return x + 0.5 * y


def _stage_weight_chunks(step, stages):
    for chunk_ref, scratch_ref, axis, nchunks in stages:
        for c in range(nchunks):
            @pl.when(step == c)
            def _(chunk_ref=chunk_ref, scratch_ref=scratch_ref, axis=axis, c=c):
                span = slice(c * CAST_CHUNK, (c + 1) * CAST_CHUNK)
                if axis == 0:
                    scratch_ref[span, :] = chunk_ref[...].astype(BF16)
                else:
                    scratch_ref[:, span] = chunk_ref[...].astype(BF16)


def _col_chunk_spec(layer, rows, nchunks):
    return pl.BlockSpec((None, rows, CAST_CHUNK), lambda i: (layer, 0, jnp.minimum(i, nchunks - 1)))


def _row_chunk_spec(layer, cols, nchunks):
    return pl.BlockSpec((None, CAST_CHUNK, cols), lambda i: (layer, jnp.minimum(i, nchunks - 1), 0))


def _token_tile(tm, width, nstage):
    return pl.BlockSpec((tm, width), lambda i: (jnp.maximum(i - nstage, 0), 0))


def _ffn_body(x_ref, g_ref, wgc_ref, wuc_ref, wdc_ref, o_ref, wg_ref, wu_ref, wd_ref):
    step = pl.program_id(0)
    nstage = wg_ref.shape[1] // CAST_CHUNK
    _stage_weight_chunks(step, ((wgc_ref, wg_ref, 1, nstage), (wuc_ref, wu_ref, 1, nstage),
                                (wdc_ref, wd_ref, 0, nstage)))

    @pl.when(step >= nstage)
    def _():
        for r in range(0, TM_FFN, FFN_SUBTILE):
            rows = slice(r, r + FFN_SUBTILE)
            o_ref[rows, :] = _swiglu_update(x_ref[rows, :], g_ref, wg_ref, wu_ref, wd_ref)


def _ffn(x, norm_g, wg, wu, wd, layer):
    n, d = x.shape
    d_ff = wg.shape[2]
    nstage = d_ff // CAST_CHUNK
    tile = _token_tile(TM_FFN, d, nstage)
    return pl.pallas_call(
        _ffn_body,
        grid=(nstage + n // TM_FFN,),
        in_specs=[tile, _const_spec((1, d)), _col_chunk_spec(layer, d, nstage),
                  _col_chunk_spec(layer, d, nstage), _row_chunk_spec(layer, d, nstage)],
        out_specs=tile,
        out_shape=jax.ShapeDtypeStruct((n, d), F32),
        scratch_shapes=[pltpu.VMEM((d, d_ff), BF16), pltpu.VMEM((d, d_ff), BF16),
                        pltpu.VMEM((d_ff, d), BF16)],
        compiler_params=_params("arbitrary"),
        name="ffn",
    )(x, norm_g, wg, wu, wd)


def _log_sigmoid(x):
    return jnp.minimum(x, 0.0) - jnp.log(1.0 + jnp.exp(-jnp.abs(x)))


def _inproj_body(x_ref, g_ref, wmc_ref, wc_ref, wgate_ref, bgate_ref,
                 naq_ref, nak_ref, nav_ref, gq_ref, gk_ref, gv_ref, gr_ref, lg_ref, wm_ref):
    step = pl.program_id(0)
    nstage = wm_ref.shape[1] // CAST_CHUNK
    _stage_weight_chunks(step, ((wmc_ref, wm_ref, 1, nstage),))
    pl.when(step >= nstage)(functools.partial(
        _inproj_tile, x_ref, g_ref, wm_ref, wc_ref, wgate_ref, bgate_ref,
        naq_ref, nak_ref, nav_ref, gq_ref, gk_ref, gv_ref, gr_ref, lg_ref))


def _inproj_tile(x_ref, g_ref, wm_ref, wc_ref, wgate_ref, bgate_ref,
                 naq_ref, nak_ref, nav_ref, gq_ref, gk_ref, gv_ref, gr_ref, lg_ref):
    outs = ((naq_ref, NA_WIDTH, NA_HEAD_DIM ** -0.5), (nak_ref, NA_WIDTH, None),
            (nav_ref, NA_WIDTH, None), (gq_ref, GLA_QK_WIDTH, GLA_DK ** -0.5),
            (gk_ref, GLA_QK_WIDTH, None), (gv_ref, GLA_V_WIDTH, None), (gr_ref, GLA_V_WIDTH, None))
    wgate = wgate_ref[...].astype(BF16)
    for r in range(0, TM_PROJ, PROJ_SUBTILE):
        rows = slice(r, r + PROJ_SUBTILE)
        xb = _rmsnorm(x_ref[rows, :], g_ref[...]).astype(BF16)
        codes = _dot(xb, wc_ref[...])
        logits = _dot(codes.astype(BF16), wgate) + bgate_ref[...]
        lg_ref[rows, :] = _log_sigmoid(logits) * (1.0 / GLA_TAU)
        col = 0
        pending = list(outs)
        while pending:
            group, width = [], 0
            while pending and width + pending[0][1] <= PROJ_DOT_WIDTH:
                group.append(pending.pop(0))
                width += group[-1][1]
            p = _dot(xb, wm_ref[:, col:col + width])
            col += width
            off = 0
            for ref, w, scale in group:
                piece = p[:, off:off + w]
                if scale is not None:
                    piece = piece * scale
                ref[rows, :] = piece.astype(ref.dtype)
                off += w


def _inproj(x, norm_g, w_in, layer, w_code, w_gate, b_gate):
    n, d = x.shape
    tm = TM_PROJ
    nstage = MAIN_WIDTH // CAST_CHUNK

    def tile(width):
        return _token_tile(tm, width, nstage)

    widths = (NA_WIDTH, NA_WIDTH, NA_WIDTH, GLA_QK_WIDTH, GLA_QK_WIDTH, GLA_V_WIDTH,
              GLA_V_WIDTH, 2 * GLA_QK_WIDTH)
    dtypes = (BF16, BF16, BF16, F32, F32, BF16, F32, F32)
    return pl.pallas_call(
        _inproj_body,
        grid=(nstage + n // tm,),
        in_specs=[tile(d), _const_spec((1, d)), _col_chunk_spec(layer, d, nstage),
                  _const_spec(w_code.shape), _const_spec(w_gate.shape), _const_spec(b_gate.shape)],
        out_specs=[tile(w) for w in widths],
        out_shape=[jax.ShapeDtypeStruct((n, w), dt) for w, dt in zip(widths, dtypes)],
        scratch_shapes=[pltpu.VMEM((d, MAIN_WIDTH), BF16)],
        compiler_params=_params("arbitrary"),
        name="inproj",
    )(x, norm_g, w_in, w_code, w_gate, b_gate)


_NA_ROW_PAIRS = 2 * WIN_H - 2


def _na_bias_table(rpb):
    qc = np.arange(GRID_W)
    kc = np.arange(GRID_W)
    col_start = np.clip(qc - WIN_W // 2, 0, GRID_W - WIN_W)
    valid = (kc[None, :] >= col_start[:, None]) & (kc[None, :] < col_start[:, None] + WIN_W)
    dc = np.clip(kc[None, :] - qc[:, None] + (WIN_W - 1), 0, 2 * WIN_W - 2)
    onehot = (dc[:, :, None] == np.arange(2 * WIN_W - 1)[None, None, :]).astype(np.float32)
    t = jnp.einsum("hrc,qkc->hrqk", rpb.astype(F32), onehot, precision=lax.Precision.HIGHEST)
    t = jnp.where(valid[None, None], t, NEG_INF)
    return jnp.concatenate([t[:, :-1], t[:, 1:]], axis=-1)


def _na_body(q_ref, k_ref, v_ref, bias_ref, o_ref, s_ref, *, nrows):
    rb = pl.program_id(2)
    half = WIN_H // 2
    lane = lax.broadcasted_iota(jnp.int32, (GRID_W, LANES), 1)
    first_head = lane < NA_HEAD_DIM
    nkeys = WIN_H * GRID_W

    def key_offset(i):
        r = rb * NA_ROWS_PER_STEP + i
        row_start = jnp.clip(r - half, 0, nrows - WIN_H)
        return r, row_start, pl.multiple_of(row_start * GRID_W, GRID_W)

    def scores(i):
        _, _, koff = key_offset(i)
        qr = q_ref[0, i * GRID_W:(i + 1) * GRID_W, :]
        zero = jnp.zeros_like(qr)
        qs = jnp.concatenate([jnp.where(first_head, qr, zero), jnp.where(first_head, zero, qr)], axis=0)
        return _dot_nt(qs, k_ref[0, pl.ds(koff, nkeys), :])

    def finish(i):
        r, row_start, koff = key_offset(i)
        dr0 = row_start - r + (WIN_H - 1)
        bias = jnp.concatenate(
            [jnp.concatenate([bias_ref[hl, dr0 + 2 * wp] for wp in range(WIN_H // 2)], axis=1)
             for hl in range(2)], axis=0)
        s = s_ref[i % (NA_LOOKAHEAD + 1)] + bias
        e = jnp.exp(s - jnp.max(s, axis=-1, keepdims=True))
        o = _dot(e.astype(BF16), v_ref[0, pl.ds(koff, nkeys), :])
        o = o / jnp.sum(e, axis=-1, keepdims=True)
        o_ref[0, i * GRID_W:(i + 1) * GRID_W, :] = jnp.where(first_head, o[:GRID_W], o[GRID_W:])

    for i in range(NA_LOOKAHEAD):
        s_ref[i] = scores(i)
    for i in range(NA_ROWS_PER_STEP):
        ahead = i + NA_LOOKAHEAD
        if ahead < NA_ROWS_PER_STEP:
            s_ref[ahead % (NA_LOOKAHEAD + 1)] = scores(ahead)
        finish(i)


def _na(q, k, v, bias):
    b, s, _ = q.shape
    nrows = s // GRID_W
    qrows = NA_ROWS_PER_STEP * GRID_W
    return pl.pallas_call(
        functools.partial(_na_body, nrows=nrows),
        grid=(NA_HEADS // 2, b, nrows // NA_ROWS_PER_STEP),
        in_specs=[
            pl.BlockSpec((1, qrows, LANES), lambda hp, bi, rb: (bi, rb, hp)),
            pl.BlockSpec((1, s, LANES), lambda hp, bi, rb: (bi, 0, hp)),
            pl.BlockSpec((1, s, LANES), lambda hp, bi, rb: (bi, 0, hp)),
            pl.BlockSpec((2, _NA_ROW_PAIRS, GRID_W, LANES), lambda hp, bi, rb: (hp, 0, 0, 0)),
        ],
        out_specs=pl.BlockSpec((1, qrows, LANES), lambda hp, bi, rb: (bi, rb, hp)),
        out_shape=jax.ShapeDtypeStruct((b, s, NA_WIDTH), F32),
        scratch_shapes=[pltpu.VMEM((NA_LOOKAHEAD + 1, 2 * GRID_W, WIN_H * GRID_W), F32)],
        compiler_params=_params("parallel", "parallel", "arbitrary"),
        name="na",
    )(q, k, v, bias)


def _split3(x):
    x1 = x.astype(BF16)
    r1 = x - x1.astype(F32)
    x2 = r1.astype(BF16)
    x3 = (r1 - x2.astype(F32)).astype(BF16)
    return x1, x2, x3


class _GlaSubBlock:
    def __init__(self, q_ref, k_ref, v_ref, g_ref, o_ref, s_ref, b_ref, a_ref, u_ref, *,
                 r0, slot, tri, a_mask, last, order):
        self.q_ref, self.k_ref, self.v_ref, self.g_ref = q_ref, k_ref, v_ref, g_ref
        self.o_ref, self.s_ref, self.b_ref, self.a_ref, self.u_ref = o_ref, s_ref, b_ref, a_ref, u_ref
        self.r0, self.slot = r0, slot
        self.rows = slice(r0, r0 + GLA_SUB)
        self.tri, self.a_mask, self.last, self.order = tri, a_mask, last, order

    def cumulative_decay(self):
        g = self.g_ref[0, self.rows, :]
        self.b_ref[self.slot] = sum(_dot(self.tri, part) for part in _split3(g))

    def scores(self, head_masks):
        b = self.b_ref[self.slot]
        self.qt = (self.q_ref[0, self.rows, :] * jnp.exp(b)).astype(BF16)
        kt = (self.k_ref[0, self.rows, :] * jnp.exp(-b)).astype(BF16)
        self.qs = {}
        for c in range(GLA_SUB // GLA_CHUNK):
            rows = slice(c * GLA_CHUNK, (c + 1) * GLA_CHUNK)
            qc = self.qt[rows]
            qs = jnp.concatenate([jnp.where(hm, qc, jnp.zeros_like(qc)) for hm in head_masks], axis=0)
            a = _dot_nt(qs, kt[rows])
            self.a_ref[self.slot, c] = jnp.where(self.a_mask, a, 0.0).astype(BF16)
            self.qs[c] = qs

    def increments(self, chunk_masks):
        nchunk = GLA_SUB // GLA_CHUNK
        b = self.b_ref[self.slot]
        b_last = jnp.concatenate(
            [jnp.broadcast_to(b[c * GLA_CHUNK + self.last:c * GLA_CHUNK + self.last + 1, :],
                              (GLA_CHUNK, GLA_QK_WIDTH)) for c in range(nchunk)], axis=0)
        ku_t = (self.k_ref[0, self.rows, :] * jnp.exp(b_last - b)).T.astype(BF16)
        self.decay_t = jnp.exp(b_last.T)
        for c in range(nchunk):
            pair = slice((c // 2) * 2 * GLA_CHUNK, (c // 2 + 1) * 2 * GLA_CHUNK)
            vrows = slice(self.r0 + pair.start, self.r0 + pair.stop)
            ku_pair = ku_t[:, pair]
            ku_c = jnp.where(chunk_masks[c % 2], ku_pair, jnp.zeros_like(ku_pair))
            for h in range(GLA_HEADS):
                hrows = slice(h * GLA_DK, (h + 1) * GLA_DK)
                vcols = slice(h * GLA_DV, (h + 1) * GLA_DV)
                self.u_ref[self.slot, c, hrows, :] = _dot(ku_c[hrows], self.v_ref[0, vrows, vcols])

    def advance_state(self):
        s = self.s_ref[...]
        self.states = {}
        for c in self.order:
            self.states[c] = s.astype(BF16)
            col = c * GLA_CHUNK
            s = self.decay_t[:, col:col + 1] * s + self.u_ref[self.slot, c]
        self.s_ref[...] = s

    def output_chunk(self, c):
        rows = slice(self.r0 + c * GLA_CHUNK, self.r0 + (c + 1) * GLA_CHUNK)
        inter = _dot(self.qs[c], self.states[c])
        for h in range(GLA_HEADS):
            arows = slice(h * GLA_CHUNK, (h + 1) * GLA_CHUNK)
            vcols = slice(h * GLA_DV, (h + 1) * GLA_DV)
            intra = _dot(self.a_ref[self.slot, c, arows, :], self.v_ref[0, rows, vcols])
            self.o_ref[0, rows, vcols] = inter[arows] + intra


def _gla_body(qf_ref, kf_ref, vf_ref, gf_ref, qb_ref, kb_ref, vb_ref, gb_ref,
              of_ref, ob_ref, sf_ref, sb_ref, bf_ref, bb_ref, af_ref, ab_ref, uf_ref, ub_ref):
    @pl.when(pl.program_id(1) == 0)
    def _():
        sf_ref[...] = jnp.zeros_like(sf_ref)
        sb_ref[...] = jnp.zeros_like(sb_ref)

    shift = GLA_CHUNK.bit_length() - 1
    row = lax.broadcasted_iota(jnp.int32, (GLA_SUB, GLA_SUB), 0)
    col = lax.broadcasted_iota(jnp.int32, (GLA_SUB, GLA_SUB), 1)
    same_chunk = (row >> shift) == (col >> shift)
    lower = jnp.where(same_chunk & (row >= col), 1.0, 0.0).astype(BF16)
    upper = jnp.where(same_chunk & (row <= col), 1.0, 0.0).astype(BF16)
    nchunk = GLA_SUB // GLA_CHUNK
    nsub = GLA_BLOCK // GLA_SUB
    lane = lax.broadcasted_iota(jnp.int32, (GLA_CHUNK, GLA_QK_WIDTH), 1)
    head_masks = [(lane >> shift) == h for h in range(GLA_HEADS)]
    pair_lane = lax.broadcasted_iota(jnp.int32, (GLA_QK_WIDTH, 2 * GLA_CHUNK), 1)
    chunk_masks = [(pair_lane >> shift) == j for j in range(2)]
    t = lax.broadcasted_iota(jnp.int32, (GLA_HEADS * GLA_CHUNK, GLA_CHUNK), 0) & (GLA_CHUNK - 1)
    s = lax.broadcasted_iota(jnp.int32, (GLA_HEADS * GLA_CHUNK, GLA_CHUNK), 1)
    fwd = [_GlaSubBlock(qf_ref, kf_ref, vf_ref, gf_ref, of_ref, sf_ref, bf_ref, af_ref, uf_ref,
                        r0=i * GLA_SUB, slot=i, tri=lower, a_mask=t >= s, last=GLA_CHUNK - 1,
                        order=tuple(range(nchunk))) for i in range(nsub)]
    bwd = [_GlaSubBlock(qb_ref, kb_ref, vb_ref, gb_ref, ob_ref, sb_ref, bb_ref, ab_ref, ub_ref,
                        r0=i * GLA_SUB, slot=i, tri=upper, a_mask=t < s, last=0,
                        order=tuple(reversed(range(nchunk)))) for i in reversed(range(nsub))]
    both = [blk for pair in zip(fwd, bwd) for blk in pair]
    for blk in both:
        blk.cumulative_decay()
    for blk in both:
        blk.scores(head_masks)
    for blk in both:
        blk.increments(chunk_masks)
    for blk in both:
        blk.advance_state()
    for f, b in zip(fwd, bwd):
        for cf, cb in zip(f.order, b.order):
            f.output_chunk(cf)
            b.output_chunk(cb)


def _gla(q, k, v, lg):
    b, s, _ = q.shape
    nb = s // GLA_BLOCK

    def fwd(width, j=0):
        return pl.BlockSpec((1, GLA_BLOCK, width), lambda bi, n: (bi, n, j))

    def bwd(width, j=0):
        return pl.BlockSpec((1, GLA_BLOCK, width), lambda bi, n: (bi, nb - 1 - n, j))

    nsub = GLA_BLOCK // GLA_SUB
    nchunk = GLA_SUB // GLA_CHUNK
    state = pltpu.VMEM((GLA_QK_WIDTH, GLA_DV), F32)
    decay = pltpu.VMEM((nsub, GLA_SUB, GLA_QK_WIDTH), F32)
    weights = pltpu.VMEM((nsub, nchunk, GLA_HEADS * GLA_CHUNK, GLA_CHUNK), BF16)
    increments = pltpu.VMEM((nsub, nchunk, GLA_QK_WIDTH, GLA_DV), F32)
    return pl.pallas_call(
        _gla_body,
        grid=(b, nb),
        in_specs=[fwd(GLA_QK_WIDTH), fwd(GLA_QK_WIDTH), fwd(GLA_V_WIDTH), fwd(GLA_QK_WIDTH, 0),
                  bwd(GLA_QK_WIDTH), bwd(GLA_QK_WIDTH), bwd(GLA_V_WIDTH), bwd(GLA_QK_WIDTH, 1)],
        out_specs=[fwd(GLA_V_WIDTH), bwd(GLA_V_WIDTH)],
        out_shape=[jax.ShapeDtypeStruct((b, s, GLA_V_WIDTH), F32)] * 2,
        scratch_shapes=[state, state, decay, decay, weights, weights, increments, increments],
        compiler_params=_params("parallel", "arbitrary"),
        name="gla",
    )(q, k, v, lg, q, k, v, lg)


def _mix_update(x, na_ref, nag_ref, of_ref, ob_ref, gg_ref, gr_ref, wo_ref):
    na = _rmsnorm(na_ref[...], nag_ref[...]).astype(BF16)
    y = _dot(na, wo_ref[:NA_WIDTH, :])
    o = of_ref[...] + ob_ref[...]
    r = gr_ref[...]
    gate = r * jax.nn.sigmoid(r)
    heads = []
    for h in range(GLA_HEADS):
        cols = slice(h * GLA_DV, (h + 1) * GLA_DV)
        heads.append((_rmsnorm(o[:, cols], gg_ref[...]) * gate[:, cols]).astype(BF16))
    return x + y + _dot(jnp.concatenate(heads, axis=-1), wo_ref[NA_WIDTH:, :])


def _mix_ffn_body(final_norm, x_ref, na_ref, nag_ref, of_ref, ob_ref, gg_ref, gr_ref, woc_ref,
                  g_ref, wgc_ref, wuc_ref, wdc_ref, fg_ref, o_ref, wo_ref, wg_ref, wu_ref, wd_ref):
    step = pl.program_id(0)
    nstage = wg_ref.shape[1] // CAST_CHUNK
    _stage_weight_chunks(step, ((woc_ref, wo_ref, 1, wo_ref.shape[1] // CAST_CHUNK),
                                (wgc_ref, wg_ref, 1, nstage), (wuc_ref, wu_ref, 1, nstage),
                                (wdc_ref, wd_ref, 0, nstage)))

    @pl.when(step >= nstage)
    def _():
        x = _mix_update(x_ref[...], na_ref, nag_ref, of_ref, ob_ref, gg_ref, gr_ref, wo_ref)
        y = _swiglu_update(x, g_ref, wg_ref, wu_ref, wd_ref)
        o_ref[...] = _rmsnorm(y, fg_ref[...]) if final_norm else y


def _mix_ffn(x, na_o, na_gain, o_f, o_b, gla_gain, g_r, w_out, norm_g, wg, wu, wd, layer,
             final_g, final_norm):
    n, d = x.shape
    d_ff = wg.shape[2]
    tm = TM_PROJ
    nstage = d_ff // CAST_CHUNK

    def tile(width):
        return _token_tile(tm, width, nstage)

    return pl.pallas_call(
        functools.partial(_mix_ffn_body, final_norm),
        grid=(nstage + n // tm,),
        in_specs=[tile(d), tile(NA_WIDTH), _const_spec((1, NA_WIDTH)), tile(GLA_V_WIDTH),
                  tile(GLA_V_WIDTH), _const_spec((1, GLA_DV)), tile(GLA_V_WIDTH),
                  _col_chunk_spec(layer, w_out.shape[1], d // CAST_CHUNK),
                  _const_spec((1, d)), _col_chunk_spec(layer, d, nstage),
                  _col_chunk_spec(layer, d, nstage), _row_chunk_spec(layer, d, nstage),
                  _const_spec((1, d))],
        out_specs=tile(d),
        out_shape=jax.ShapeDtypeStruct((n, d), F32),
        scratch_shapes=[pltpu.VMEM(w_out.shape[1:], BF16), pltpu.VMEM((d, d_ff), BF16),
                        pltpu.VMEM((d, d_ff), BF16), pltpu.VMEM((d_ff, d), BF16)],
        compiler_params=_params("arbitrary"),
        name="mix_ffn_final" if final_norm else "mix_ffn",
    )(x, na_o, na_gain, o_f, o_b, gla_gain, g_r, w_out, norm_g, wg, wu, wd, final_g)


def _gate_weights(w_f, b_f, w_b, b_b):
    w = jnp.zeros((CODE_PAD, 2 * GLA_QK_WIDTH), F32)
    w = w.at[:GLA_GATE_RANK, :GLA_QK_WIDTH].set(w_f)
    w = w.at[GLA_GATE_RANK:2 * GLA_GATE_RANK, GLA_QK_WIDTH:].set(w_b)
    return w, jnp.concatenate([b_f, b_b])[None, :]


def kernel(x, ffn1_norm, ffn1_wg, ffn1_wu, ffn1_wd, mix_norm, w_in, na_rpb, na_gain, w_gate_f, b_gate_f, w_gate_b, b_gate_b, gla_gain, w_out, ffn2_norm, ffn2_wg, ffn2_wu, ffn2_wd, final_norm):
    bsz, seq, d = x.shape
    depth = ffn1_norm.shape[0]
    n = bsz * seq
    xt = x.reshape(n, d)
    for l in range(depth):
        xt = _ffn(xt, ffn1_norm[l][None], ffn1_wg, ffn1_wu, ffn1_wd, l)
        w_code = jnp.pad(w_in[l][:, MAIN_WIDTH:], ((0, 0), (0, CODE_PAD - 2 * GLA_GATE_RANK))).astype(BF16)
        w_gate, b_gate = _gate_weights(w_gate_f[l], b_gate_f[l], w_gate_b[l], b_gate_b[l])
        na_q, na_k, na_v, g_q, g_k, g_v, g_r, lg = _inproj(xt, mix_norm[l][None], w_in, l, w_code,
                                                           w_gate, b_gate)
        to3 = lambda a: a.reshape(bsz, seq, a.shape[-1])
        na_o = _na(to3(na_q), to3(na_k), to3(na_v), _na_bias_table(na_rpb[l]))
        o_f, o_b = _gla(to3(g_q), to3(g_k), to3(g_v), to3(lg))
        xt = _mix_ffn(xt, na_o.reshape(n, NA_WIDTH), na_gain[l][None], o_f.reshape(n, GLA_V_WIDTH),
                      o_b.reshape(n, GLA_V_WIDTH), gla_gain[l][None], g_r, w_out, ffn2_norm[l][None],
                      ffn2_wg, ffn2_wu, ffn2_wd, l, final_norm[None], final_norm=l == depth - 1)
    return xt.reshape(bsz, seq, d)
```

```python
import functools

import numpy as np
import jax
import jax.numpy as jnp
from jax import lax
from jax.experimental import pallas as pl
from jax.experimental.pallas import tpu as pltpu

F32 = jnp.float32
BF16 = jnp.bfloat16

EPS = 1e-6
NEG_INF = -1e30

GRID_W = 64
WIN_H = 8
WIN_W = 16
NA_HEADS = 8
NA_HEAD_DIM = 64
NA_WIDTH = NA_HEADS * NA_HEAD_DIM
GLA_HEADS = 4
GLA_DK = 64
GLA_DV = 128
GLA_QK_WIDTH = GLA_HEADS * GLA_DK
GLA_V_WIDTH = GLA_HEADS * GLA_DV
GLA_GATE_RANK = 16
GLA_TAU = 16.0
GLA_CHUNK = 64
MAIN_WIDTH = 3 * NA_WIDTH + 2 * GLA_QK_WIDTH + 2 * GLA_V_WIDTH

LANES = 128
CODE_PAD = LANES
VMEM_LIMIT = 56 * 1024 * 1024

TM_TOKENS = 512
TOKEN_SUBTILE = 256
PROJ_DOT_WIDTH = 1024
CAST_CHUNK = 256
NA_ROWS_PER_STEP = 32
NA_LOOKAHEAD = 3
GLA_BLOCK = 512
GLA_SUB = 256


def _rmsnorm(x, g):
    return x * lax.rsqrt(jnp.mean(x * x, axis=-1, keepdims=True) + EPS) * g


def _dot(a, b):
    return jnp.dot(a, b, preferred_element_type=F32)


def _dot_nt(a, b):
    return lax.dot_general(a, b, (((1,), (1,)), ((), ())), preferred_element_type=F32)


def _const_spec(shape):
    return pl.BlockSpec(shape, lambda *_: (0,) * len(shape), pipeline_mode=pl.Buffered(1))


def _params(*sem):
    return pltpu.CompilerParams(dimension_semantics=sem, vmem_limit_bytes=VMEM_LIMIT)


def _ffn_chunks(d_ff):
    chunks, start = [], 0
    while start < d_ff:
        width = min(1024, d_ff - start)
        chunks.append((start, width))
        start += width
    return tuple(chunks)


def _swiglu_update(x, g_ref, wg_ref, wu_ref, wd_ref):
    xb = _rmsnorm(x, g_ref[...]).astype(BF16)
    y = None
    for start, width in _ffn_chunks(wg_ref.shape[1]):
        h = _dot(xb, wg_ref[:, start:start + width])
        u = _dot(xb, wu_ref[:, start:start + width])
        a = (h * jax.nn.sigmoid(h) * u).astype(BF16)
        part = _dot(a, wd_ref[start:start + width, :])
        y = part if y is None else y + part
    return x + 0.5 * y


def _stage_weight_chunks(step, stages):
    for chunk_ref, scratch_ref, axis, nchunks in stages:
        for c in range(nchunks):
            @pl.when(step == c)
            def _(chunk_ref=chunk_ref, scratch_ref=scratch_ref, axis=axis, c=c):
                span = slice(c * CAST_CHUNK, (c + 1) * CAST_CHUNK)
                if axis == 0:
                    scratch_ref[span, :] = chunk_ref[...].astype(BF16)
                else:
                    scratch_ref[:, span] = chunk_ref[...].astype(BF16)


def _col_chunk_spec(layer, rows, nchunks):
    return pl.BlockSpec((None, rows, CAST_CHUNK), lambda i: (layer, 0, jnp.minimum(i, nchunks - 1)))


def _row_chunk_spec(layer, cols, nchunks):
    return pl.BlockSpec((None, CAST_CHUNK, cols), lambda i: (layer, jnp.minimum(i, nchunks - 1), 0))


def _token_tile(tm, width, nstage):
    return pl.BlockSpec((tm, width), lambda i: (jnp.maximum(i - nstage, 0), 0))


def _log_sigmoid(x):
    return jnp.minimum(x, 0.0) - jnp.log(1.0 + jnp.exp(-jnp.abs(x)))


def _project_rows(x, rows, g_ref, wm_ref, wc_ref, wgate, bgate_ref, outs, lg_ref):
    xb = _rmsnorm(x, g_ref[...]).astype(BF16)
    codes = _dot(xb, wc_ref[...])
    logits = _dot(codes.astype(BF16), wgate) + bgate_ref[...]
    lg_ref[rows, :] = _log_sigmoid(logits) * (1.0 / GLA_TAU)
    col = 0
    pending = list(outs)
    while pending:
        group, width = [], 0
        while pending and width + pending[0][1] <= PROJ_DOT_WIDTH:
            group.append(pending.pop(0))
            width += group[-1][1]
        p = _dot(xb, wm_ref[:, col:col + width])
        col += width
        off = 0
        for ref, w, scale in group:
            piece = p[:, off:off + w]
            if scale is not None:
                piece = piece * scale
            ref[rows, :] = piece.astype(ref.dtype)
            off += w


def _ffn_inproj_body(x_ref, g1_ref, wgc_ref, wuc_ref, wdc_ref, g2_ref, wmc_ref, wc_ref,
                     wgate_ref, bgate_ref,
                     xo_ref, naq_ref, nak_ref, nav_ref, gq_ref, gk_ref, gv_ref, gr_ref, lg_ref,
                     wg_ref, wu_ref, wd_ref, wm_ref):
    step = pl.program_id(0)
    nffn = wg_ref.shape[1] // CAST_CHUNK
    nproj = wm_ref.shape[1] // CAST_CHUNK
    _stage_weight_chunks(step, ((wgc_ref, wg_ref, 1, nffn), (wuc_ref, wu_ref, 1, nffn),
                                (wdc_ref, wd_ref, 0, nffn), (wmc_ref, wm_ref, 1, nproj)))

    @pl.when(step >= max(nffn, nproj))
    def _():
        outs = ((naq_ref, NA_WIDTH, NA_HEAD_DIM ** -0.5), (nak_ref, NA_WIDTH, None),
                (nav_ref, NA_WIDTH, None), (gq_ref, GLA_QK_WIDTH, GLA_DK ** -0.5),
                (gk_ref, GLA_QK_WIDTH, None), (gv_ref, GLA_V_WIDTH, None),
                (gr_ref, GLA_V_WIDTH, None))
        wgate = wgate_ref[...].astype(BF16)
        subtiles = [slice(r, r + TOKEN_SUBTILE) for r in range(0, TM_TOKENS, TOKEN_SUBTILE)]
        for rows in subtiles:
            xo_ref[rows, :] = _swiglu_update(x_ref[rows, :], g1_ref, wg_ref, wu_ref, wd_ref)
        for rows in subtiles:
            _project_rows(xo_ref[rows, :], rows, g2_ref, wm_ref, wc_ref, wgate, bgate_ref, outs, lg_ref)


def _ffn_inproj(x, ffn_g, wg, wu, wd, mix_g, w_in, w_code, w_gate, b_gate, layer):
    n, d = x.shape
    d_ff = wg.shape[2]
    nffn = d_ff // CAST_CHUNK
    nproj = MAIN_WIDTH // CAST_CHUNK
    nstage = max(nffn, nproj)

    def tile(width):
        return _token_tile(TM_TOKENS, width, nstage)

    widths = (d, NA_WIDTH, NA_WIDTH, NA_WIDTH, GLA_QK_WIDTH, GLA_QK_WIDTH, GLA_V_WIDTH,
              GLA_V_WIDTH, 2 * GLA_QK_WIDTH)
    dtypes = (F32, BF16, BF16, BF16, F32, F32, BF16, F32, F32)
    return pl.pallas_call(
        _ffn_inproj_body,
        grid=(nstage + n // TM_TOKENS,),
        in_specs=[tile(d), _const_spec((1, d)), _col_chunk_spec(layer, d, nffn),
                  _col_chunk_spec(layer, d, nffn), _row_chunk_spec(layer, d, nffn),
                  _const_spec((1, d)), _col_chunk_spec(layer, d, nproj),
                  _const_spec(w_code.shape), _const_spec(w_gate.shape), _const_spec(b_gate.shape)],
        out_specs=[tile(w) for w in widths],
        out_shape=[jax.ShapeDtypeStruct((n, w), dt) for w, dt in zip(widths, dtypes)],
        scratch_shapes=[pltpu.VMEM((d, d_ff), BF16), pltpu.VMEM((d, d_ff), BF16),
                        pltpu.VMEM((d_ff, d), BF16), pltpu.VMEM((d, MAIN_WIDTH), BF16)],
        compiler_params=_params("arbitrary"),
        name="ffn_inproj",
    )(x, ffn_g, wg, wu, wd, mix_g, w_in, w_code, w_gate, b_gate)


_NA_ROW_PAIRS = 2 * WIN_H - 2


def _na_bias_tables(rpb):
    ncol = 2 * WIN_W - 1
    qc = np.arange(GRID_W)
    kc = np.arange(GRID_W)
    col_start = np.clip(qc - WIN_W // 2, 0, GRID_W - WIN_W)
    valid = (kc[None, :] >= col_start[:, None]) & (kc[None, :] < col_start[:, None] + WIN_W)
    dc = np.clip(kc[None, :] - qc[:, None] + (WIN_W - 1), 0, ncol - 1)
    onehot = (np.arange(ncol)[:, None, None] == dc[None]).astype(np.float32)
    selector = np.einsum("ab,cqk->acqbk", np.eye(2, dtype=np.float32), onehot)
    selector = selector.reshape(2 * ncol, GRID_W * 2 * GRID_W)
    pairs = jnp.concatenate([rpb[:, :, :-1], rpb[:, :, 1:]], axis=-1).astype(F32)
    lead = pairs.shape[:-1]
    t = jnp.dot(pairs.reshape(-1, 2 * ncol), selector, precision=lax.Precision.HIGHEST)
    t = t.reshape(*lead, GRID_W, 2 * GRID_W)
    return jnp.where(np.concatenate([valid, valid], axis=-1), t, NEG_INF)


def _na_body(q_ref, k_ref, v_ref, bias_ref, o_ref, s_ref, *, nrows):
    rb = pl.program_id(2)
    half = WIN_H // 2
    lane = lax.broadcasted_iota(jnp.int32, (GRID_W, LANES), 1)
    first_head = lane < NA_HEAD_DIM
    nkeys = WIN_H * GRID_W

    def key_offset(i):
        r = rb * NA_ROWS_PER_STEP + i
        row_start = jnp.clip(r - half, 0, nrows - WIN_H)
        return r, row_start, pl.multiple_of(row_start * GRID_W, GRID_W)

    def scores(i):
        _, _, koff = key_offset(i)
        qr = q_ref[0, i * GRID_W:(i + 1) * GRID_W, :]
        zero = jnp.zeros_like(qr)
        qs = jnp.concatenate([jnp.where(first_head, qr, zero), jnp.where(first_head, zero, qr)], axis=0)
        return _dot_nt(qs, k_ref[0, pl.ds(koff, nkeys), :])

    def finish(i):
        r, row_start, koff = key_offset(i)
        dr0 = row_start - r + (WIN_H - 1)
        bias = jnp.concatenate(
            [jnp.concatenate([bias_ref[hl, dr0 + 2 * wp] for wp in range(WIN_H // 2)], axis=1)
             for hl in range(2)], axis=0)
        s = s_ref[i % (NA_LOOKAHEAD + 1)] + bias
        e = jnp.exp(s - jnp.max(s, axis=-1, keepdims=True))
        o = _dot(e.astype(BF16), v_ref[0, pl.ds(koff, nkeys), :])
        o = o / jnp.sum(e, axis=-1, keepdims=True)
        o_ref[0, i * GRID_W:(i + 1) * GRID_W, :] = jnp.where(first_head, o[:GRID_W], o[GRID_W:])

    for i in range(NA_LOOKAHEAD):
        s_ref[i] = scores(i)
    for i in range(NA_ROWS_PER_STEP):
        ahead = i + NA_LOOKAHEAD
        if ahead < NA_ROWS_PER_STEP:
            s_ref[ahead % (NA_LOOKAHEAD + 1)] = scores(ahead)
        finish(i)


def _na(q, k, v, bias):
    b, s, _ = q.shape
    nrows = s // GRID_W
    qrows = NA_ROWS_PER_STEP * GRID_W
    return pl.pallas_call(
        functools.partial(_na_body, nrows=nrows),
        grid=(NA_HEADS // 2, b, nrows // NA_ROWS_PER_STEP),
        in_specs=[
            pl.BlockSpec((1, qrows, LANES), lambda hp, bi, rb: (bi, rb, hp)),
            pl.BlockSpec((1, s, LANES), lambda hp, bi, rb: (bi, 0, hp)),
            pl.BlockSpec((1, s, LANES), lambda hp, bi, rb: (bi, 0, hp)),
            pl.BlockSpec((2, _NA_ROW_PAIRS, GRID_W, LANES), lambda hp, bi, rb: (hp, 0, 0, 0)),
        ],
        out_specs=pl.BlockSpec((1, qrows, LANES), lambda hp, bi, rb: (bi, rb, hp)),
        out_shape=jax.ShapeDtypeStruct((b, s, NA_WIDTH), F32),
        scratch_shapes=[pltpu.VMEM((NA_LOOKAHEAD + 1, 2 * GRID_W, WIN_H * GRID_W), F32)],
        compiler_params=_params("parallel", "parallel", "arbitrary"),
        name="na",
    )(q, k, v, bias)


def _split3(x):
    x1 = x.astype(BF16)
    r1 = x - x1.astype(F32)
    x2 = r1.astype(BF16)
    x3 = (r1 - x2.astype(F32)).astype(BF16)
    return x1, x2, x3


class _GlaSubBlock:
    def __init__(self, q_ref, k_ref, v_ref, g_ref, o_ref, s_ref, b_ref, a_ref, u_ref, *,
                 r0, slot, tri, a_mask, last, order):
        self.q_ref, self.k_ref, self.v_ref, self.g_ref = q_ref, k_ref, v_ref, g_ref
        self.o_ref, self.s_ref, self.b_ref, self.a_ref, self.u_ref = o_ref, s_ref, b_ref, a_ref, u_ref
        self.r0, self.slot = r0, slot
        self.rows = slice(r0, r0 + GLA_SUB)
        self.tri, self.a_mask, self.last, self.order = tri, a_mask, last, order

    def cumulative_decay(self):
        g = self.g_ref[0, self.rows, :]
        self.b_ref[self.slot] = sum(_dot(self.tri, part) for part in _split3(g))

    def scores(self, head_masks):
        b = self.b_ref[self.slot]
        self.qt = (self.q_ref[0, self.rows, :] * jnp.exp(b)).astype(BF16)
        kt = (self.k_ref[0, self.rows, :] * jnp.exp(-b)).astype(BF16)
        self.qs = {}
        for c in range(GLA_SUB // GLA_CHUNK):
            rows = slice(c * GLA_CHUNK, (c + 1) * GLA_CHUNK)
            qc = self.qt[rows]
            qs = jnp.concatenate([jnp.where(hm, qc, jnp.zeros_like(qc)) for hm in head_masks], axis=0)
            a = _dot_nt(qs, kt[rows])
            self.a_ref[self.slot, c] = jnp.where(self.a_mask, a, 0.0).astype(BF16)
            self.qs[c] = qs

    def increments(self, chunk_masks):
        nchunk = GLA_SUB // GLA_CHUNK
        b = self.b_ref[self.slot]
        b_last = jnp.concatenate(
            [jnp.broadcast_to(b[c * GLA_CHUNK + self.last:c * GLA_CHUNK + self.last + 1, :],
                              (GLA_CHUNK, GLA_QK_WIDTH)) for c in range(nchunk)], axis=0)
        ku_t = (self.k_ref[0, self.rows, :] * jnp.exp(b_last - b)).T.astype(BF16)
        self.decay_t = jnp.exp(b_last.T)
        for c in range(nchunk):
            pair = slice((c // 2) * 2 * GLA_CHUNK, (c // 2 + 1) * 2 * GLA_CHUNK)
            vrows = slice(self.r0 + pair.start, self.r0 + pair.stop)
            ku_pair = ku_t[:, pair]
            ku_c = jnp.where(chunk_masks[c % 2], ku_pair, jnp.zeros_like(ku_pair))
            for h in range(GLA_HEADS):
                hrows = slice(h * GLA_DK, (h + 1) * GLA_DK)
                vcols = slice(h * GLA_DV, (h + 1) * GLA_DV)
                self.u_ref[self.slot, c, hrows, :] = _dot(ku_c[hrows], self.v_ref[0, vrows, vcols])

    def advance_state(self):
        s = self.s_ref[...]
        self.states = {}
        for c in self.order:
            self.states[c] = s.astype(BF16)
            col = c * GLA_CHUNK
            s = self.decay_t[:, col:col + 1] * s + self.u_ref[self.slot, c]
        self.s_ref[...] = s

    def output_chunk(self, c):
        rows = slice(self.r0 + c * GLA_CHUNK, self.r0 + (c + 1) * GLA_CHUNK)
        inter = _dot(self.qs[c], self.states[c])
        for h in range(GLA_HEADS):
            arows = slice(h * GLA_CHUNK, (h + 1) * GLA_CHUNK)
            vcols = slice(h * GLA_DV, (h + 1) * GLA_DV)
            intra = _dot(self.a_ref[self.slot, c, arows, :], self.v_ref[0, rows, vcols])
            self.o_ref[0, rows, vcols] = inter[arows] + intra


def _gla_body(qf_ref, kf_ref, vf_ref, gf_ref, qb_ref, kb_ref, vb_ref, gb_ref,
              of_ref, ob_ref, sf_ref, sb_ref, bf_ref, bb_ref, af_ref, ab_ref, uf_ref, ub_ref):
    @pl.when(pl.program_id(1) == 0)
    def _():
        sf_ref[...] = jnp.zeros_like(sf_ref)
        sb_ref[...] = jnp.zeros_like(sb_ref)

    shift = GLA_CHUNK.bit_length() - 1
    row = lax.broadcasted_iota(jnp.int32, (GLA_SUB, GLA_SUB), 0)
    col = lax.broadcasted_iota(jnp.int32, (GLA_SUB, GLA_SUB), 1)
    same_chunk = (row >> shift) == (col >> shift)
    lower = jnp.where(same_chunk & (row >= col), 1.0, 0.0).astype(BF16)
    upper = jnp.where(same_chunk & (row <= col), 1.0, 0.0).astype(BF16)
    nchunk = GLA_SUB // GLA_CHUNK
    nsub = GLA_BLOCK // GLA_SUB
    lane = lax.broadcasted_iota(jnp.int32, (GLA_CHUNK, GLA_QK_WIDTH), 1)
    head_masks = [(lane >> shift) == h for h in range(GLA_HEADS)]
    pair_lane = lax.broadcasted_iota(jnp.int32, (GLA_QK_WIDTH, 2 * GLA_CHUNK), 1)
    chunk_masks = [(pair_lane >> shift) == j for j in range(2)]
    t = lax.broadcasted_iota(jnp.int32, (GLA_HEADS * GLA_CHUNK, GLA_CHUNK), 0) & (GLA_CHUNK - 1)
    s = lax.broadcasted_iota(jnp.int32, (GLA_HEADS * GLA_CHUNK, GLA_CHUNK), 1)
    fwd = [_GlaSubBlock(qf_ref, kf_ref, vf_ref, gf_ref, of_ref, sf_ref, bf_ref, af_ref, uf_ref,
                        r0=i * GLA_SUB, slot=i, tri=lower, a_mask=t >= s, last=GLA_CHUNK - 1,
                        order=tuple(range(nchunk))) for i in range(nsub)]
    bwd = [_GlaSubBlock(qb_ref, kb_ref, vb_ref, gb_ref, ob_ref, sb_ref, bb_ref, ab_ref, ub_ref,
                        r0=i * GLA_SUB, slot=i, tri=upper, a_mask=t < s, last=0,
                        order=tuple(reversed(range(nchunk)))) for i in reversed(range(nsub))]
    both = [blk for pair in zip(fwd, bwd) for blk in pair]
    for blk in both:
        blk.cumulative_decay()
    for blk in both:
        blk.scores(head_masks)
    for blk in both:
        blk.increments(chunk_masks)
    for blk in both:
        blk.advance_state()
    for f, b in zip(fwd, bwd):
        for cf, cb in zip(f.order, b.order):
            f.output_chunk(cf)
            b.output_chunk(cb)


def _gla(q, k, v, lg):
    b, s, _ = q.shape
    nb = s // GLA_BLOCK

    def fwd(width, j=0):
        return pl.BlockSpec((1, GLA_BLOCK, width), lambda bi, n: (bi, n, j))

    def bwd(width, j=0):
        return pl.BlockSpec((1, GLA_BLOCK, width), lambda bi, n: (bi, nb - 1 - n, j))

    nsub = GLA_BLOCK // GLA_SUB
    nchunk = GLA_SUB // GLA_CHUNK
    state = pltpu.VMEM((GLA_QK_WIDTH, GLA_DV), F32)
    decay = pltpu.VMEM((nsub, GLA_SUB, GLA_QK_WIDTH), F32)
    weights = pltpu.VMEM((nsub, nchunk, GLA_HEADS * GLA_CHUNK, GLA_CHUNK), BF16)
    increments = pltpu.VMEM((nsub, nchunk, GLA_QK_WIDTH, GLA_DV), F32)
    return pl.pallas_call(
        _gla_body,
        grid=(b, nb),
        in_specs=[fwd(GLA_QK_WIDTH), fwd(GLA_QK_WIDTH), fwd(GLA_V_WIDTH), fwd(GLA_QK_WIDTH, 0),
                  bwd(GLA_QK_WIDTH), bwd(GLA_QK_WIDTH), bwd(GLA_V_WIDTH), bwd(GLA_QK_WIDTH, 1)],
        out_specs=[fwd(GLA_V_WIDTH), bwd(GLA_V_WIDTH)],
        out_shape=[jax.ShapeDtypeStruct((b, s, GLA_V_WIDTH), F32)] * 2,
        scratch_shapes=[state, state, decay, decay, weights, weights, increments, increments],
        compiler_params=_params("parallel", "arbitrary"),
        name="gla",
    )(q, k, v, lg, q, k, v, lg)


def _mix_update(x, rows, na_ref, nag_ref, of_ref, ob_ref, gg_ref, gr_ref, wo_ref):
    na = _rmsnorm(na_ref[rows, :], nag_ref[...]).astype(BF16)
    y = _dot(na, wo_ref[:NA_WIDTH, :])
    o = of_ref[rows, :] + ob_ref[rows, :]
    r = gr_ref[rows, :]
    gate = r * jax.nn.sigmoid(r)
    heads = []
    for h in range(GLA_HEADS):
        cols = slice(h * GLA_DV, (h + 1) * GLA_DV)
        heads.append((_rmsnorm(o[:, cols], gg_ref[...]) * gate[:, cols]).astype(BF16))
    return x + y + _dot(jnp.concatenate(heads, axis=-1), wo_ref[NA_WIDTH:, :])


def _mix_ffn_body(final_norm, x_ref, na_ref, nag_ref, of_ref, ob_ref, gg_ref, gr_ref, woc_ref,
                  g_ref, wgc_ref, wuc_ref, wdc_ref, fg_ref, o_ref, wo_ref, wg_ref, wu_ref, wd_ref):
    step = pl.program_id(0)
    nstage = wg_ref.shape[1] // CAST_CHUNK
    _stage_weight_chunks(step, ((woc_ref, wo_ref, 1, wo_ref.shape[1] // CAST_CHUNK),
                                (wgc_ref, wg_ref, 1, nstage), (wuc_ref, wu_ref, 1, nstage),
                                (wdc_ref, wd_ref, 0, nstage)))

    @pl.when(step >= nstage)
    def _():
        subtiles = [slice(r, r + TOKEN_SUBTILE) for r in range(0, TM_TOKENS, TOKEN_SUBTILE)]
        for rows in subtiles:
            o_ref[rows, :] = _mix_update(x_ref[rows, :], rows, na_ref, nag_ref, of_ref, ob_ref,
                                         gg_ref, gr_ref, wo_ref)
        for rows in subtiles:
            y = _swiglu_update(o_ref[rows, :], g_ref, wg_ref, wu_ref, wd_ref)
            o_ref[rows, :] = _rmsnorm(y, fg_ref[...]) if final_norm else y


def _mix_ffn(x, na_o, na_gain, o_f, o_b, gla_gain, g_r, w_out, norm_g, wg, wu, wd, layer,
             final_g, final_norm):
    n, d = x.shape
    d_ff = wg.shape[2]
    tm = TM_TOKENS
    nstage = d_ff // CAST_CHUNK

    def tile(width):
        return _token_tile(tm, width, nstage)

    return pl.pallas_call(
        functools.partial(_mix_ffn_body, final_norm),
        grid=(nstage + n // tm,),
        in_specs=[tile(d), tile(NA_WIDTH), _const_spec((1, NA_WIDTH)), tile(GLA_V_WIDTH),
                  tile(GLA_V_WIDTH), _const_spec((1, GLA_DV)), tile(GLA_V_WIDTH),
                  _col_chunk_spec(layer, w_out.shape[1], d // CAST_CHUNK),
                  _const_spec((1, d)), _col_chunk_spec(layer, d, nstage),
                  _col_chunk_spec(layer, d, nstage), _row_chunk_spec(layer, d, nstage),
                  _const_spec((1, d))],
        out_specs=tile(d),
        out_shape=jax.ShapeDtypeStruct((n, d), F32),
        scratch_shapes=[pltpu.VMEM(w_out.shape[1:], BF16), pltpu.VMEM((d, d_ff), BF16),
                        pltpu.VMEM((d, d_ff), BF16), pltpu.VMEM((d_ff, d), BF16)],
        compiler_params=_params("arbitrary"),
        name="mix_ffn_final" if final_norm else "mix_ffn",
    )(x, na_o, na_gain, o_f, o_b, gla_gain, g_r, w_out, norm_g, wg, wu, wd, final_g)


def _gate_weights(w_f, b_f, w_b, b_b):
    w = jnp.zeros((CODE_PAD, 2 * GLA_QK_WIDTH), F32)
    w = w.at[:GLA_GATE_RANK, :GLA_QK_WIDTH].set(w_f)
    w = w.at[GLA_GATE_RANK:2 * GLA_GATE_RANK, GLA_QK_WIDTH:].set(w_b)
    return w, jnp.concatenate([b_f, b_b])[None, :]


def kernel(x, ffn1_norm, ffn1_wg, ffn1_wu, ffn1_wd, mix_norm, w_in, na_rpb, na_gain, w_gate_f, b_gate_f, w_gate_b, b_gate_b, gla_gain, w_out, ffn2_norm, ffn2_wg, ffn2_wu, ffn2_wd, final_norm):
    bsz, seq, d = x.shape
    depth = ffn1_norm.shape[0]
    n = bsz * seq
    xt = x.reshape(n, d)
    bias = _na_bias_tables(na_rpb)
    for l in range(depth):
        w_code = jnp.pad(w_in[l][:, MAIN_WIDTH:], ((0, 0), (0, CODE_PAD - 2 * GLA_GATE_RANK))).astype(BF16)
        w_gate, b_gate = _gate_weights(w_gate_f[l], b_gate_f[l], w_gate_b[l], b_gate_b[l])
        xt, na_q, na_k, na_v, g_q, g_k, g_v, g_r, lg = _ffn_inproj(
            xt, ffn1_norm[l][None], ffn1_wg, ffn1_wu, ffn1_wd, mix_norm[l][None], w_in, w_code,
            w_gate, b_gate, l)
        to3 = lambda a: a.reshape(bsz, seq, a.shape[-1])
        na_o = _na(to3(na_q), to3(na_k), to3(na_v), bias[l])
        o_f, o_b = _gla(to3(g_q), to3(g_k), to3(g_v), to3(lg))
        xt = _mix_ffn(xt, na_o.reshape(n, NA_WIDTH), na_gain[l][None], o_f.reshape(n, GLA_V_WIDTH),
                      o_b.reshape(n, GLA_V_WIDTH), gla_gain[l][None], g_r, w_out, ffn2_norm[l][None],
                      ffn2_wg, ffn2_wu, ffn2_wd, l, final_norm[None], final_norm=l == depth - 1)
    return xt.reshape(bsz, seq, d)
```

```python
import functools

import numpy as np
import jax
import jax.numpy as jnp
from jax import lax
from jax.experimental import pallas as pl
from jax.experimental.pallas import tpu as pltpu

F32 = jnp.float32
BF16 = jnp.bfloat16

EPS = 1e-6
NEG_INF = -1e30

GRID_W = 64
WIN_H = 8
WIN_W = 16
NA_HEADS = 8
NA_HEAD_DIM = 64
NA_WIDTH = NA_HEADS * NA_HEAD_DIM
GLA_HEADS = 4
GLA_DK = 64
GLA_DV = 128
GLA_QK_WIDTH = GLA_HEADS * GLA_DK
GLA_V_WIDTH = GLA_HEADS * GLA_DV
GLA_GATE_RANK = 16
GLA_TAU = 16.0
GLA_CHUNK = 64
MAIN_WIDTH = 3 * NA_WIDTH + 2 * GLA_QK_WIDTH + 2 * GLA_V_WIDTH

LANES = 128
SUBLANES = 8
GLA_SAFE_DECAY = 60.0
CODE_PAD = LANES
VMEM_LIMIT = 56 * 1024 * 1024

TM_TOKENS = 512
TOKEN_SUBTILE = 256
PROJ_DOT_WIDTH = 1024
CAST_CHUNK = 256
NA_ROWS_PER_STEP = 32
NA_LOOKAHEAD = 3
GLA_BLOCK = 512
GLA_SUB = 256


def _rmsnorm(x, g):
    return x * lax.rsqrt(jnp.mean(x * x, axis=-1, keepdims=True) + EPS) * g


def _dot(a, b):
    return jnp.dot(a, b, preferred_element_type=F32)


def _dot_nt(a, b):
    return lax.dot_general(a, b, (((1,), (1,)), ((), ())), preferred_element_type=F32)


def _const_spec(shape):
    return pl.BlockSpec(shape, lambda *_: (0,) * len(shape), pipeline_mode=pl.Buffered(1))


def _params(*sem):
    return pltpu.CompilerParams(dimension_semantics=sem, vmem_limit_bytes=VMEM_LIMIT)


def _ffn_chunks(d_ff):
    chunks, start = [], 0
    while start < d_ff:
        width = min(1024, d_ff - start)
        chunks.append((start, width))
        start += width
    return tuple(chunks)


def _swiglu_update(x, g_ref, wg_ref, wu_ref, wd_ref):
    xb = _rmsnorm(x, g_ref[...]).astype(BF16)
    y = None
    for start, width in _ffn_chunks(wg_ref.shape[1]):
        h = _dot(xb, wg_ref[:, start:start + width])
        u = _dot(xb, wu_ref[:, start:start + width])
        a = (h * jax.nn.sigmoid(h) * u).astype(BF16)
        part = _dot(a, wd_ref[start:start + width, :])
        y = part if y is None else y + part
    return x + 0.5 * y


def _stage_weight_chunks(step, stages):
    for chunk_ref, scratch_ref, axis, nchunks in stages:
        for c in range(nchunks):
            @pl.when(step == c)
            def _(chunk_ref=chunk_ref, scratch_ref=scratch_ref, axis=axis, c=c):
                span = slice(c * CAST_CHUNK, (c + 1) * CAST_CHUNK)
                if axis == 0:
                    scratch_ref[span, :] = chunk_ref[...].astype(BF16)
                else:
                    scratch_ref[:, span] = chunk_ref[...].astype(BF16)


def _col_chunk_spec(layer, rows, nchunks):
    return pl.BlockSpec((None, rows, CAST_CHUNK), lambda i: (layer, 0, jnp.minimum(i, nchunks - 1)))


def _row_chunk_spec(layer, cols, nchunks):
    return pl.BlockSpec((None, CAST_CHUNK, cols), lambda i: (layer, jnp.minimum(i, nchunks - 1), 0))


def _token_tile(tm, width, nstage):
    return pl.BlockSpec((tm, width), lambda i: (jnp.maximum(i - nstage, 0), 0))


def _log_sigmoid(x):
    return jnp.minimum(x, 0.0) - jnp.log(1.0 + jnp.exp(-jnp.abs(x)))


def _project_rows(x, rows, g_ref, wm_ref, wc_ref, wgate, bgate_ref, outs, lg_ref):
    xb = _rmsnorm(x, g_ref[...]).astype(BF16)
    codes = _dot(xb, wc_ref[...])
    logits = _dot(codes.astype(BF16), wgate) + bgate_ref[...]
    lg = _log_sigmoid(logits) * (1.0 / GLA_TAU)
    lg_ref[rows, :] = lg
    chunk_sums = [jnp.sum(lg[c:c + GLA_CHUNK], axis=0, keepdims=True)
                  for c in range(0, lg.shape[0], GLA_CHUNK)]
    min_decay = jnp.min(functools.reduce(jnp.minimum, chunk_sums), axis=1, keepdims=True)
    col = 0
    pending = list(outs)
    while pending:
        group, width = [], 0
        while pending and width + pending[0][1] <= PROJ_DOT_WIDTH:
            group.append(pending.pop(0))
            width += group[-1][1]
        p = _dot(xb, wm_ref[:, col:col + width])
        col += width
        off = 0
        for ref, w, scale in group:
            piece = p[:, off:off + w]
            if scale is not None:
                piece = piece * scale
            ref[rows, :] = piece.astype(ref.dtype)
            off += w
    return min_decay


def _ffn_inproj_body(x_ref, g1_ref, wgc_ref, wuc_ref, wdc_ref, g2_ref, wmc_ref, wc_ref,
                     wgate_ref, bgate_ref,
                     xo_ref, naq_ref, nak_ref, nav_ref, gq_ref, gk_ref, gv_ref, gr_ref, lg_ref,
                     md_ref, wg_ref, wu_ref, wd_ref, wm_ref):
    step = pl.program_id(0)
    nffn = wg_ref.shape[1] // CAST_CHUNK
    nproj = wm_ref.shape[1] // CAST_CHUNK
    _stage_weight_chunks(step, ((wgc_ref, wg_ref, 1, nffn), (wuc_ref, wu_ref, 1, nffn),
                                (wdc_ref, wd_ref, 0, nffn), (wmc_ref, wm_ref, 1, nproj)))

    @pl.when(step >= max(nffn, nproj))
    def _():
        outs = ((naq_ref, NA_WIDTH, NA_HEAD_DIM ** -0.5), (nak_ref, NA_WIDTH, None),
                (nav_ref, NA_WIDTH, None), (gq_ref, GLA_QK_WIDTH, GLA_DK ** -0.5),
                (gk_ref, GLA_QK_WIDTH, None), (gv_ref, GLA_V_WIDTH, None),
                (gr_ref, GLA_V_WIDTH, None))
        wgate = wgate_ref[...].astype(BF16)
        subtiles = [slice(r, r + TOKEN_SUBTILE) for r in range(0, TM_TOKENS, TOKEN_SUBTILE)]
        for rows in subtiles:
            xo_ref[rows, :] = _swiglu_update(x_ref[rows, :], g1_ref, wg_ref, wu_ref, wd_ref)
        mins = [_project_rows(xo_ref[rows, :], rows, g2_ref, wm_ref, wc_ref, wgate, bgate_ref,
                              outs, lg_ref) for rows in subtiles]
        md_ref[...] = jnp.broadcast_to(functools.reduce(jnp.minimum, mins), md_ref.shape)


def _ffn_inproj(x, ffn_g, wg, wu, wd, mix_g, w_in, w_code, w_gate, b_gate, layer):
    n, d = x.shape
    d_ff = wg.shape[2]
    nffn = d_ff // CAST_CHUNK
    nproj = MAIN_WIDTH // CAST_CHUNK
    nstage = max(nffn, nproj)

    def tile(width):
        return _token_tile(TM_TOKENS, width, nstage)

    widths = (d, NA_WIDTH, NA_WIDTH, NA_WIDTH, GLA_QK_WIDTH, GLA_QK_WIDTH, GLA_V_WIDTH,
              GLA_V_WIDTH, 2 * GLA_QK_WIDTH)
    dtypes = (F32, BF16, BF16, BF16, F32, F32, BF16, F32, F32)
    return pl.pallas_call(
        _ffn_inproj_body,
        grid=(nstage + n // TM_TOKENS,),
        in_specs=[tile(d), _const_spec((1, d)), _col_chunk_spec(layer, d, nffn),
                  _col_chunk_spec(layer, d, nffn), _row_chunk_spec(layer, d, nffn),
                  _const_spec((1, d)), _col_chunk_spec(layer, d, nproj),
                  _const_spec(w_code.shape), _const_spec(w_gate.shape), _const_spec(b_gate.shape)],
        out_specs=[tile(w) for w in widths] + [_token_tile(SUBLANES, LANES, nstage)],
        out_shape=[jax.ShapeDtypeStruct((n, w), dt) for w, dt in zip(widths, dtypes)]
        + [jax.ShapeDtypeStruct((n // TM_TOKENS * SUBLANES, LANES), F32)],
        scratch_shapes=[pltpu.VMEM((d, d_ff), BF16), pltpu.VMEM((d, d_ff), BF16),
                        pltpu.VMEM((d_ff, d), BF16), pltpu.VMEM((d, MAIN_WIDTH), BF16)],
        compiler_params=_params("arbitrary"),
        name="ffn_inproj",
    )(x, ffn_g, wg, wu, wd, mix_g, w_in, w_code, w_gate, b_gate)


_NA_ROW_PAIRS = 2 * WIN_H - 2


def _na_bias_tables(rpb):
    ncol = 2 * WIN_W - 1
    qc = np.arange(GRID_W)
    kc = np.arange(GRID_W)
    col_start = np.clip(qc - WIN_W // 2, 0, GRID_W - WIN_W)
    valid = (kc[None, :] >= col_start[:, None]) & (kc[None, :] < col_start[:, None] + WIN_W)
    dc = np.clip(kc[None, :] - qc[:, None] + (WIN_W - 1), 0, ncol - 1)
    onehot = (np.arange(ncol)[:, None, None] == dc[None]).astype(np.float32)
    selector = np.einsum("ab,cqk->acqbk", np.eye(2, dtype=np.float32), onehot)
    selector = selector.reshape(2 * ncol, GRID_W * 2 * GRID_W)
    pairs = jnp.concatenate([rpb[:, :, :-1], rpb[:, :, 1:]], axis=-1).astype(F32)
    lead = pairs.shape[:-1]
    t = jnp.dot(pairs.reshape(-1, 2 * ncol), selector, precision=lax.Precision.HIGHEST)
    t = t.reshape(*lead, GRID_W, 2 * GRID_W)
    return jnp.where(np.concatenate([valid, valid], axis=-1), t, NEG_INF)


def _na_body(q_ref, k_ref, v_ref, bias_ref, o_ref, s_ref, *, nrows):
    rb = pl.program_id(2)
    half = WIN_H // 2
    lane = lax.broadcasted_iota(jnp.int32, (GRID_W, LANES), 1)
    first_head = lane < NA_HEAD_DIM
    nkeys = WIN_H * GRID_W

    def key_offset(i):
        r = rb * NA_ROWS_PER_STEP + i
        row_start = jnp.clip(r - half, 0, nrows - WIN_H)
        return r, row_start, pl.multiple_of(row_start * GRID_W, GRID_W)

    def scores(i):
        _, _, koff = key_offset(i)
        qr = q_ref[0, i * GRID_W:(i + 1) * GRID_W, :]
        zero = jnp.zeros_like(qr)
        qs = jnp.concatenate([jnp.where(first_head, qr, zero), jnp.where(first_head, zero, qr)], axis=0)
        return _dot_nt(qs, k_ref[0, pl.ds(koff, nkeys), :])

    def finish(i):
        r, row_start, koff = key_offset(i)
        dr0 = row_start - r + (WIN_H - 1)
        bias = jnp.concatenate(
            [jnp.concatenate([bias_ref[hl, dr0 + 2 * wp] for wp in range(WIN_H // 2)], axis=1)
             for hl in range(2)], axis=0)
        s = s_ref[i % (NA_LOOKAHEAD + 1)] + bias
        e = jnp.exp(s - jnp.max(s, axis=-1, keepdims=True))
        o = _dot(e.astype(BF16), v_ref[0, pl.ds(koff, nkeys), :])
        o = o / jnp.sum(e, axis=-1, keepdims=True)
        o_ref[0, i * GRID_W:(i + 1) * GRID_W, :] = jnp.where(first_head, o[:GRID_W], o[GRID_W:])

    for i in range(NA_LOOKAHEAD):
        s_ref[i] = scores(i)
    for i in range(NA_ROWS_PER_STEP):
        ahead = i + NA_LOOKAHEAD
        if ahead < NA_ROWS_PER_STEP:
            s_ref[ahead % (NA_LOOKAHEAD + 1)] = scores(ahead)
        finish(i)


def _na(q, k, v, bias):
    b, s, _ = q.shape
    nrows = s // GRID_W
    qrows = NA_ROWS_PER_STEP * GRID_W
    return pl.pallas_call(
        functools.partial(_na_body, nrows=nrows),
        grid=(NA_HEADS // 2, b, nrows // NA_ROWS_PER_STEP),
        in_specs=[
            pl.BlockSpec((1, qrows, LANES), lambda hp, bi, rb: (bi, rb, hp)),
            pl.BlockSpec((1, s, LANES), lambda hp, bi, rb: (bi, 0, hp)),
            pl.BlockSpec((1, s, LANES), lambda hp, bi, rb: (bi, 0, hp)),
            pl.BlockSpec((2, _NA_ROW_PAIRS, GRID_W, LANES), lambda hp, bi, rb: (hp, 0, 0, 0)),
        ],
        out_specs=pl.BlockSpec((1, qrows, LANES), lambda hp, bi, rb: (bi, rb, hp)),
        out_shape=jax.ShapeDtypeStruct((b, s, NA_WIDTH), F32),
        scratch_shapes=[pltpu.VMEM((NA_LOOKAHEAD + 1, 2 * GRID_W, WIN_H * GRID_W), F32)],
        compiler_params=_params("parallel", "parallel", "arbitrary"),
        name="na",
    )(q, k, v, bias)


def _split3(x):
    x1 = x.astype(BF16)
    r1 = x - x1.astype(F32)
    x2 = r1.astype(BF16)
    x3 = (r1 - x2.astype(F32)).astype(BF16)
    return x1, x2, x3


class _GlaSubBlock:
    def __init__(self, q_ref, k_ref, v_ref, g_ref, o_ref, s_ref, b_ref, a_ref, u_ref, vrow_ref, *,
                 r0, slot, tri, a_mask, forward, exact):
        self.q_ref, self.k_ref, self.v_ref, self.g_ref = q_ref, k_ref, v_ref, g_ref
        self.o_ref, self.s_ref, self.b_ref, self.a_ref, self.u_ref = o_ref, s_ref, b_ref, a_ref, u_ref
        self.vrow_ref = vrow_ref
        self.r0, self.slot = r0, slot
        self.rows = slice(r0, r0 + GLA_SUB)
        self.tri, self.a_mask, self.forward, self.exact = tri, a_mask, forward, exact
        nchunk = GLA_SUB // GLA_CHUNK
        self.last = GLA_CHUNK - 1 if forward else 0
        self.order = tuple(range(nchunk)) if forward else tuple(reversed(range(nchunk)))

    def cumulative_decay(self):
        g = self.g_ref[0, self.rows, :]
        self.b_ref[self.slot] = sum(_dot(self.tri, part) for part in _split3(g))

    def scores(self, head_masks, head_expand):
        b = self.b_ref[self.slot]
        q = self.q_ref[0, self.rows, :]
        self.qt = (q * jnp.exp(b)).astype(BF16)
        if not self.exact:
            kt = (self.k_ref[0, self.rows, :] * jnp.exp(-b)).astype(BF16)
        self.qs = {}
        for c in range(GLA_SUB // GLA_CHUNK):
            rows = slice(c * GLA_CHUNK, (c + 1) * GLA_CHUNK)
            qc = self.qt[rows]
            qs = jnp.concatenate([jnp.where(hm, qc, jnp.zeros_like(qc)) for hm in head_masks], axis=0)
            self.qs[c] = qs
            if self.exact:
                self.a_ref[self.slot, rows, :] = self._exact_intra(c, q[rows], b[rows], head_expand)
            else:
                a = _dot_nt(qs, kt[rows])
                self.a_ref[self.slot, c] = jnp.where(self.a_mask, a, 0.0).astype(BF16)

    def _exact_intra(self, c, qc, bc, head_expand):
        base = self.r0 + c * GLA_CHUNK
        self.vrow_ref[...] = self.v_ref[0, base:base + GLA_CHUNK, :].astype(F32)
        t = lax.broadcasted_iota(jnp.int32, (GLA_CHUNK, 1), 0)

        def body(s, acc):
            krow = self.k_ref[0, pl.ds(base + s, 1), :]
            brow = self.b_ref[self.slot, pl.ds(c * GLA_CHUNK + s, 1), :]
            valid = (t >= s) if self.forward else (t < s)
            decay = jnp.exp(jnp.where(valid, bc - brow, 0.0))
            x = jnp.where(valid, qc * krow * decay, 0.0).astype(BF16)
            w = _dot(x, head_expand)
            return acc + w * self.vrow_ref[pl.ds(s, 1), :]

        return lax.fori_loop(0, GLA_CHUNK, body, jnp.zeros((GLA_CHUNK, GLA_V_WIDTH), F32))

    def increments(self, chunk_masks):
        nchunk = GLA_SUB // GLA_CHUNK
        b = self.b_ref[self.slot]
        b_last = jnp.concatenate(
            [jnp.broadcast_to(b[c * GLA_CHUNK + self.last:c * GLA_CHUNK + self.last + 1, :],
                              (GLA_CHUNK, GLA_QK_WIDTH)) for c in range(nchunk)], axis=0)
        ku_t = (self.k_ref[0, self.rows, :] * jnp.exp(b_last - b)).T.astype(BF16)
        self.decay_t = jnp.exp(b_last.T)
        for c in range(nchunk):
            pair = slice((c // 2) * 2 * GLA_CHUNK, (c // 2 + 1) * 2 * GLA_CHUNK)
            vrows = slice(self.r0 + pair.start, self.r0 + pair.stop)
            ku_pair = ku_t[:, pair]
            ku_c = jnp.where(chunk_masks[c % 2], ku_pair, jnp.zeros_like(ku_pair))
            for h in range(GLA_HEADS):
                hrows = slice(h * GLA_DK, (h + 1) * GLA_DK)
                vcols = slice(h * GLA_DV, (h + 1) * GLA_DV)
                self.u_ref[self.slot, c, hrows, :] = _dot(ku_c[hrows], self.v_ref[0, vrows, vcols])

    def advance_state(self):
        s = self.s_ref[...]
        self.states = {}
        for c in self.order:
            self.states[c] = s.astype(BF16)
            col = c * GLA_CHUNK
            s = self.decay_t[:, col:col + 1] * s + self.u_ref[self.slot, c]
        self.s_ref[...] = s

    def output_chunk(self, c):
        sub_rows = slice(c * GLA_CHUNK, (c + 1) * GLA_CHUNK)
        rows = slice(self.r0 + sub_rows.start, self.r0 + sub_rows.stop)
        inter = _dot(self.qs[c], self.states[c])
        for h in range(GLA_HEADS):
            arows = slice(h * GLA_CHUNK, (h + 1) * GLA_CHUNK)
            vcols = slice(h * GLA_DV, (h + 1) * GLA_DV)
            if self.exact:
                intra = self.a_ref[self.slot, sub_rows, vcols]
            else:
                intra = _dot(self.a_ref[self.slot, c, arows, :], self.v_ref[0, rows, vcols])
            self.o_ref[0, rows, vcols] = inter[arows] + intra


def _gla_body(exact, qf_ref, kf_ref, vf_ref, gf_ref, qb_ref, kb_ref, vb_ref, gb_ref,
              of_ref, ob_ref, sf_ref, sb_ref, bf_ref, bb_ref, af_ref, ab_ref, uf_ref, ub_ref,
              vrow_ref):
    @pl.when(pl.program_id(1) == 0)
    def _():
        sf_ref[...] = jnp.zeros_like(sf_ref)
        sb_ref[...] = jnp.zeros_like(sb_ref)

    shift = GLA_CHUNK.bit_length() - 1
    row = lax.broadcasted_iota(jnp.int32, (GLA_SUB, GLA_SUB), 0)
    col = lax.broadcasted_iota(jnp.int32, (GLA_SUB, GLA_SUB), 1)
    same_chunk = (row >> shift) == (col >> shift)
    lower = jnp.where(same_chunk & (row >= col), 1.0, 0.0).astype(BF16)
    upper = jnp.where(same_chunk & (row <= col), 1.0, 0.0).astype(BF16)
    nsub = GLA_BLOCK // GLA_SUB
    lane = lax.broadcasted_iota(jnp.int32, (GLA_CHUNK, GLA_QK_WIDTH), 1)
    head_masks = [(lane >> shift) == h for h in range(GLA_HEADS)]
    pair_lane = lax.broadcasted_iota(jnp.int32, (GLA_QK_WIDTH, 2 * GLA_CHUNK), 1)
    chunk_masks = [(pair_lane >> shift) == j for j in range(2)]
    t = lax.broadcasted_iota(jnp.int32, (GLA_HEADS * GLA_CHUNK, GLA_CHUNK), 0) & (GLA_CHUNK - 1)
    s = lax.broadcasted_iota(jnp.int32, (GLA_HEADS * GLA_CHUNK, GLA_CHUNK), 1)
    head_expand = None
    if exact:
        erow = lax.broadcasted_iota(jnp.int32, (GLA_QK_WIDTH, GLA_V_WIDTH), 0) // GLA_DK
        ecol = lax.broadcasted_iota(jnp.int32, (GLA_QK_WIDTH, GLA_V_WIDTH), 1) // GLA_DV
        head_expand = jnp.where(erow == ecol, 1.0, 0.0).astype(BF16)
    fwd = [_GlaSubBlock(qf_ref, kf_ref, vf_ref, gf_ref, of_ref, sf_ref, bf_ref, af_ref, uf_ref,
                        vrow_ref, r0=i * GLA_SUB, slot=i, tri=lower, a_mask=t >= s, forward=True,
                        exact=exact) for i in range(nsub)]
    bwd = [_GlaSubBlock(qb_ref, kb_ref, vb_ref, gb_ref, ob_ref, sb_ref, bb_ref, ab_ref, ub_ref,
                        vrow_ref, r0=i * GLA_SUB, slot=i, tri=upper, a_mask=t < s, forward=False,
                        exact=exact) for i in reversed(range(nsub))]
    both = [blk for pair in zip(fwd, bwd) for blk in pair]
    for blk in both:
        blk.cumulative_decay()
    for blk in both:
        blk.scores(head_masks, head_expand)
    for blk in both:
        blk.increments(chunk_masks)
    for blk in both:
        blk.advance_state()
    for f, b in zip(fwd, bwd):
        for cf, cb in zip(f.order, b.order):
            f.output_chunk(cf)
            b.output_chunk(cb)


def _gla(q, k, v, lg, *, exact):
    b, s, _ = q.shape
    nb = s // GLA_BLOCK

    def fwd(width, j=0):
        return pl.BlockSpec((1, GLA_BLOCK, width), lambda bi, n: (bi, n, j))

    def bwd(width, j=0):
        return pl.BlockSpec((1, GLA_BLOCK, width), lambda bi, n: (bi, nb - 1 - n, j))

    nsub = GLA_BLOCK // GLA_SUB
    nchunk = GLA_SUB // GLA_CHUNK
    state = pltpu.VMEM((GLA_QK_WIDTH, GLA_DV), F32)
    decay = pltpu.VMEM((nsub, GLA_SUB, GLA_QK_WIDTH), F32)
    if exact:
        intra = pltpu.VMEM((nsub, GLA_SUB, GLA_V_WIDTH), F32)
    else:
        intra = pltpu.VMEM((nsub, nchunk, GLA_HEADS * GLA_CHUNK, GLA_CHUNK), BF16)
    increments = pltpu.VMEM((nsub, nchunk, GLA_QK_WIDTH, GLA_DV), F32)
    vrows = pltpu.VMEM((GLA_CHUNK, GLA_V_WIDTH), F32)
    return pl.pallas_call(
        functools.partial(_gla_body, exact),
        grid=(b, nb),
        in_specs=[fwd(GLA_QK_WIDTH), fwd(GLA_QK_WIDTH), fwd(GLA_V_WIDTH), fwd(GLA_QK_WIDTH, 0),
                  bwd(GLA_QK_WIDTH), bwd(GLA_QK_WIDTH), bwd(GLA_V_WIDTH), bwd(GLA_QK_WIDTH, 1)],
        out_specs=[fwd(GLA_V_WIDTH), bwd(GLA_V_WIDTH)],
        out_shape=[jax.ShapeDtypeStruct((b, s, GLA_V_WIDTH), F32)] * 2,
        scratch_shapes=[state, state, decay, decay, intra, intra, increments, increments, vrows],
        compiler_params=_params("parallel", "arbitrary"),
        name="gla_exact" if exact else "gla",
    )(q, k, v, lg, q, k, v, lg)


def _mix_update(x, rows, na_ref, nag_ref, of_ref, ob_ref, gg_ref, gr_ref, wo_ref):
    na = _rmsnorm(na_ref[rows, :], nag_ref[...]).astype(BF16)
    y = _dot(na, wo_ref[:NA_WIDTH, :])
    o = of_ref[rows, :] + ob_ref[rows, :]
    r = gr_ref[rows, :]
    gate = r * jax.nn.sigmoid(r)
    heads = []
    for h in range(GLA_HEADS):
        cols = slice(h * GLA_DV, (h + 1) * GLA_DV)
        heads.append((_rmsnorm(o[:, cols], gg_ref[...]) * gate[:, cols]).astype(BF16))
    return x + y + _dot(jnp.concatenate(heads, axis=-1), wo_ref[NA_WIDTH:, :])


def _mix_ffn_body(final_norm, x_ref, na_ref, nag_ref, of_ref, ob_ref, gg_ref, gr_ref, woc_ref,
                  g_ref, wgc_ref, wuc_ref, wdc_ref, fg_ref, o_ref, wo_ref, wg_ref, wu_ref, wd_ref):
    step = pl.program_id(0)
    nstage = wg_ref.shape[1] // CAST_CHUNK
    _stage_weight_chunks(step, ((woc_ref, wo_ref, 1, wo_ref.shape[1] // CAST_CHUNK),
                                (wgc_ref, wg_ref, 1, nstage), (wuc_ref, wu_ref, 1, nstage),
                                (wdc_ref, wd_ref, 0, nstage)))

    @pl.when(step >= nstage)
    def _():
        subtiles = [slice(r, r + TOKEN_SUBTILE) for r in range(0, TM_TOKENS, TOKEN_SUBTILE)]
        for rows in subtiles:
            o_ref[rows, :] = _mix_update(x_ref[rows, :], rows, na_ref, nag_ref, of_ref, ob_ref,
                                         gg_ref, gr_ref, wo_ref)
        for rows in subtiles:
            y = _swiglu_update(o_ref[rows, :], g_ref, wg_ref, wu_ref, wd_ref)
            o_ref[rows, :] = _rmsnorm(y, fg_ref[...]) if final_norm else y


def _mix_ffn(x, na_o, na_gain, o_f, o_b, gla_gain, g_r, w_out, norm_g, wg, wu, wd, layer,
             final_g, final_norm):
    n, d = x.shape
    d_ff = wg.shape[2]
    tm = TM_TOKENS
    nstage = d_ff // CAST_CHUNK

    def tile(width):
        return _token_tile(tm, width, nstage)

    return pl.pallas_call(
        functools.partial(_mix_ffn_body, final_norm),
        grid=(nstage + n // tm,),
        in_specs=[tile(d), tile(NA_WIDTH), _const_spec((1, NA_WIDTH)), tile(GLA_V_WIDTH),
                  tile(GLA_V_WIDTH), _const_spec((1, GLA_DV)), tile(GLA_V_WIDTH),
                  _col_chunk_spec(layer, w_out.shape[1], d // CAST_CHUNK),
                  _const_spec((1, d)), _col_chunk_spec(layer, d, nstage),
                  _col_chunk_spec(layer, d, nstage), _row_chunk_spec(layer, d, nstage),
                  _const_spec((1, d))],
        out_specs=tile(d),
        out_shape=jax.ShapeDtypeStruct((n, d), F32),
        scratch_shapes=[pltpu.VMEM(w_out.shape[1:], BF16), pltpu.VMEM((d, d_ff), BF16),
                        pltpu.VMEM((d, d_ff), BF16), pltpu.VMEM((d_ff, d), BF16)],
        compiler_params=_params("arbitrary"),
        name="mix_ffn_final" if final_norm else "mix_ffn",
    )(x, na_o, na_gain, o_f, o_b, gla_gain, g_r, w_out, norm_g, wg, wu, wd, final_g)


def _gate_weights(w_f, b_f, w_b, b_b):
    w = jnp.zeros((CODE_PAD, 2 * GLA_QK_WIDTH), F32)
    w = w.at[:GLA_GATE_RANK, :GLA_QK_WIDTH].set(w_f)
    w = w.at[GLA_GATE_RANK:2 * GLA_GATE_RANK, GLA_QK_WIDTH:].set(w_b)
    return w, jnp.concatenate([b_f, b_b])[None, :]


def kernel(x, ffn1_norm, ffn1_wg, ffn1_wu, ffn1_wd, mix_norm, w_in, na_rpb, na_gain, w_gate_f, b_gate_f, w_gate_b, b_gate_b, gla_gain, w_out, ffn2_norm, ffn2_wg, ffn2_wu, ffn2_wd, final_norm):
    bsz, seq, d = x.shape
    depth = ffn1_norm.shape[0]
    n = bsz * seq
    xt = x.reshape(n, d)
    bias = _na_bias_tables(na_rpb)
    for l in range(depth):
        w_code = jnp.pad(w_in[l][:, MAIN_WIDTH:], ((0, 0), (0, CODE_PAD - 2 * GLA_GATE_RANK))).astype(BF16)
        w_gate, b_gate = _gate_weights(w_gate_f[l], b_gate_f[l], w_gate_b[l], b_gate_b[l])
        xt, na_q, na_k, na_v, g_q, g_k, g_v, g_r, lg, min_decay = _ffn_inproj(
            xt, ffn1_norm[l][None], ffn1_wg, ffn1_wu, ffn1_wd, mix_norm[l][None], w_in, w_code,
            w_gate, b_gate, l)
        to3 = lambda a: a.reshape(bsz, seq, a.shape[-1])
        na_o = _na(to3(na_q), to3(na_k), to3(na_v), bias[l])
        o_f, o_b = lax.cond(jnp.min(min_decay) >= -GLA_SAFE_DECAY,
                            functools.partial(_gla, exact=False), functools.partial(_gla, exact=True),
                            to3(g_q), to3(g_k), to3(g_v), to3(lg))
        xt = _mix_ffn(xt, na_o.reshape(n, NA_WIDTH), na_gain[l][None], o_f.reshape(n, GLA_V_WIDTH),
                      o_b.reshape(n, GLA_V_WIDTH), gla_gain[l][None], g_r, w_out, ffn2_norm[l][None],
                      ffn2_wg, ffn2_wu, ffn2_wd, l, final_norm[None], final_norm=l == depth - 1)
    return xt.reshape(bsz, seq, d)
```

```python
import functools

import numpy as np
import jax
import jax.numpy as jnp
from jax import lax
from jax.experimental import pallas as pl
from jax.experimental.pallas import tpu as pltpu

F32 = jnp.float32
BF16 = jnp.bfloat16

EPS = 1e-6
NEG_INF = -1e30

GRID_W = 64
WIN_H = 8
WIN_W = 16
NA_HEADS = 8
NA_HEAD_DIM = 64
NA_WIDTH = NA_HEADS * NA_HEAD_DIM
GLA_HEADS = 4
GLA_DK = 64
GLA_DV = 128
GLA_QK_WIDTH = GLA_HEADS * GLA_DK
GLA_V_WIDTH = GLA_HEADS * GLA_DV
GLA_GATE_RANK = 16
GLA_TAU = 16.0
GLA_CHUNK = 64
MAIN_WIDTH = 3 * NA_WIDTH + 2 * GLA_QK_WIDTH + 2 * GLA_V_WIDTH

LANES = 128
SUBLANES = 8
GLA_SAFE_DECAY = 60.0
CODE_PAD = LANES
VMEM_LIMIT = 56 * 1024 * 1024

TM_TOKENS = 512
TOKEN_SUBTILE = 256
PROJ_DOT_WIDTH = 1024
FFN_DOT_WIDTH = 1024
CAST_CHUNK = 256
NA_ROWS_PER_STEP = 64
NA_LOOKAHEAD = 3
GLA_BLOCK = 1024
GLA_SUB = 256


def _rmsnorm(x, g):
    return x * lax.rsqrt(jnp.mean(x * x, axis=-1, keepdims=True) + EPS) * g


def _dot(a, b):
    return jnp.dot(a, b, preferred_element_type=F32)


def _dot_nt(a, b):
    return lax.dot_general(a, b, (((1,), (1,)), ((), ())), preferred_element_type=F32)


def _const_spec(shape):
    return pl.BlockSpec(shape, lambda *_: (0,) * len(shape), pipeline_mode=pl.Buffered(1))


def _params(*sem):
    return pltpu.CompilerParams(dimension_semantics=sem, vmem_limit_bytes=VMEM_LIMIT)


def _ffn_chunks(d_ff):
    chunks, start = [], 0
    while start < d_ff:
        width = min(FFN_DOT_WIDTH, d_ff - start)
        chunks.append((start, width))
        start += width
    return tuple(chunks)


def _swiglu_update(x, g_ref, wg_ref, wu_ref, wd_ref):
    xb = _rmsnorm(x, g_ref[...]).astype(BF16)
    y = None
    for start, width in _ffn_chunks(wg_ref.shape[1]):
        h = _dot(xb, wg_ref[:, start:start + width])
        u = _dot(xb, wu_ref[:, start:start + width])
        a = (h * jax.nn.sigmoid(h) * u).astype(BF16)
        part = _dot(a, wd_ref[start:start + width, :])
        y = part if y is None else y + part
    return x + 0.5 * y


def _stage_weight_chunks(step, stages):
    for chunk_ref, scratch_ref, axis, nchunks in stages:
        for c in range(nchunks):
            @pl.when(step == c)
            def _(chunk_ref=chunk_ref, scratch_ref=scratch_ref, axis=axis, c=c):
                span = slice(c * CAST_CHUNK, (c + 1) * CAST_CHUNK)
                if axis == 0:
                    scratch_ref[span, :] = chunk_ref[...].astype(BF16)
                elif axis == 1:
                    scratch_ref[:, span] = chunk_ref[...].astype(BF16)
                else:
                    scratch_ref[:, span] = chunk_ref[...].T.astype(BF16)


def _col_chunk_spec(layer, rows, nchunks):
    return pl.BlockSpec((None, rows, CAST_CHUNK), lambda i: (layer, 0, jnp.minimum(i, nchunks - 1)))


def _row_chunk_spec(layer, cols, nchunks):
    return pl.BlockSpec((None, CAST_CHUNK, cols), lambda i: (layer, jnp.minimum(i, nchunks - 1), 0))


def _token_tile(tm, width, nstage):
    return pl.BlockSpec((tm, width), lambda i: (jnp.maximum(i - nstage, 0), 0))


def _log_sigmoid(x):
    return jnp.minimum(x, 0.0) - jnp.log(1.0 + jnp.exp(-jnp.abs(x)))


def _project_rows(x, rows, g_ref, wm_ref, wc_ref, wgate, bgate_ref, outs, lg_ref):
    xb = _rmsnorm(x, g_ref[...]).astype(BF16)
    codes = _dot(xb, wc_ref[...])
    logits = _dot(codes.astype(BF16), wgate) + bgate_ref[...]
    lg = _log_sigmoid(logits) * (1.0 / GLA_TAU)
    lg_ref[rows, :] = lg
    chunk_sums = [jnp.sum(lg[c:c + GLA_CHUNK], axis=0, keepdims=True)
                  for c in range(0, lg.shape[0], GLA_CHUNK)]
    min_decay = jnp.min(functools.reduce(jnp.minimum, chunk_sums), axis=1, keepdims=True)
    col = 0
    pending = list(outs)
    while pending:
        group, width = [], 0
        while pending and width + pending[0][1] <= PROJ_DOT_WIDTH:
            group.append(pending.pop(0))
            width += group[-1][1]
        p = _dot(xb, wm_ref[:, col:col + width])
        col += width
        off = 0
        for ref, w, scale in group:
            piece = p[:, off:off + w]
            if scale is not None:
                piece = piece * scale
            ref[rows, :] = piece.astype(ref.dtype)
            off += w
    return min_decay


def _ffn_inproj_body(x_ref, g1_ref, wgc_ref, wuc_ref, wdc_ref, g2_ref, wmc_ref, wc_ref,
                     wgate_ref, bgate_ref,
                     xo_ref, naq_ref, nak_ref, nav_ref, gq_ref, gk_ref, gv_ref, gr_ref, lg_ref,
                     md_ref, wg_ref, wu_ref, wd_ref, wm_ref):
    step = pl.program_id(0)
    nffn = wg_ref.shape[1] // CAST_CHUNK
    nproj = wm_ref.shape[1] // CAST_CHUNK
    _stage_weight_chunks(step, ((wgc_ref, wg_ref, 1, nffn), (wuc_ref, wu_ref, 1, nffn),
                                (wdc_ref, wd_ref, 0, nffn), (wmc_ref, wm_ref, "1t", nproj)))

    @pl.when(step >= max(nffn, nproj))
    def _():
        outs = ((naq_ref, NA_WIDTH, NA_HEAD_DIM ** -0.5), (nak_ref, NA_WIDTH, None),
                (nav_ref, NA_WIDTH, None), (gq_ref, GLA_QK_WIDTH, GLA_DK ** -0.5),
                (gk_ref, GLA_QK_WIDTH, None), (gv_ref, GLA_V_WIDTH, None),
                (gr_ref, GLA_V_WIDTH, None))
        wgate = wgate_ref[...].astype(BF16)
        subtiles = [slice(r, r + TOKEN_SUBTILE) for r in range(0, TM_TOKENS, TOKEN_SUBTILE)]
        for rows in subtiles:
            xo_ref[rows, :] = _swiglu_update(x_ref[rows, :], g1_ref, wg_ref, wu_ref, wd_ref)
        mins = [_project_rows(xo_ref[rows, :], rows, g2_ref, wm_ref, wc_ref, wgate, bgate_ref,
                              outs, lg_ref) for rows in subtiles]
        md_ref[...] = jnp.broadcast_to(functools.reduce(jnp.minimum, mins), md_ref.shape)


def _ffn_inproj(x, ffn_g, wg, wu, wd, mix_g, w_in_t, w_code, w_gate, b_gate, layer):
    n, d = x.shape
    d_ff = wg.shape[2]
    nffn = d_ff // CAST_CHUNK
    nproj = MAIN_WIDTH // CAST_CHUNK
    nstage = max(nffn, nproj)

    def tile(width):
        return _token_tile(TM_TOKENS, width, nstage)

    widths = (d, NA_WIDTH, NA_WIDTH, NA_WIDTH, GLA_QK_WIDTH, GLA_QK_WIDTH, GLA_V_WIDTH,
              GLA_V_WIDTH, 2 * GLA_QK_WIDTH)
    dtypes = (F32, BF16, BF16, BF16, F32, F32, BF16, F32, F32)
    return pl.pallas_call(
        _ffn_inproj_body,
        grid=(nstage + n // TM_TOKENS,),
        in_specs=[tile(d), _const_spec((1, d)), _col_chunk_spec(layer, d, nffn),
                  _col_chunk_spec(layer, d, nffn), _row_chunk_spec(layer, d, nffn),
                  _const_spec((1, d)), _row_chunk_spec(layer, d, nproj),
                  _const_spec(w_code.shape), _const_spec(w_gate.shape), _const_spec(b_gate.shape)],
        out_specs=[tile(w) for w in widths] + [_token_tile(SUBLANES, LANES, nstage)],
        out_shape=[jax.ShapeDtypeStruct((n, w), dt) for w, dt in zip(widths, dtypes)]
        + [jax.ShapeDtypeStruct((n // TM_TOKENS * SUBLANES, LANES), F32)],
        scratch_shapes=[pltpu.VMEM((d, d_ff), BF16), pltpu.VMEM((d, d_ff), BF16),
                        pltpu.VMEM((d_ff, d), BF16), pltpu.VMEM((d, MAIN_WIDTH), BF16)],
        compiler_params=_params("arbitrary"),
        name="ffn_inproj",
    )(x, ffn_g, wg, wu, wd, mix_g, w_in_t, w_code, w_gate, b_gate)


_NA_ROW_PAIRS = 2 * WIN_H - 2


def _na_bias_tables(rpb):
    ncol = 2 * WIN_W - 1
    qc = np.arange(GRID_W)
    kc = np.arange(GRID_W)
    col_start = np.clip(qc - WIN_W // 2, 0, GRID_W - WIN_W)
    valid = (kc[None, :] >= col_start[:, None]) & (kc[None, :] < col_start[:, None] + WIN_W)
    dc = np.clip(kc[None, :] - qc[:, None] + (WIN_W - 1), 0, ncol - 1)
    onehot = (np.arange(ncol)[:, None, None] == dc[None]).astype(np.float32)
    selector = np.einsum("ab,cqk->acqbk", np.eye(2, dtype=np.float32), onehot)
    selector = selector.reshape(2 * ncol, GRID_W * 2 * GRID_W)
    pairs = jnp.concatenate([rpb[:, :, :-1], rpb[:, :, 1:]], axis=-1).astype(F32)
    lead = pairs.shape[:-1]
    t = jnp.dot(pairs.reshape(-1, 2 * ncol), selector, precision=lax.Precision.HIGHEST)
    t = t.reshape(*lead, GRID_W, 2 * GRID_W)
    return jnp.where(np.concatenate([valid, valid], axis=-1), t, NEG_INF)


def _na_body(q_ref, k_ref, v_ref, bias_ref, o_ref, s_ref, *, nrows):
    rb = pl.program_id(2)
    half = WIN_H // 2
    lane = lax.broadcasted_iota(jnp.int32, (GRID_W, LANES), 1)
    first_head = lane < NA_HEAD_DIM
    nkeys = WIN_H * GRID_W

    def key_offset(i):
        r = rb * NA_ROWS_PER_STEP + i
        row_start = jnp.clip(r - half, 0, nrows - WIN_H)
        return r, row_start, pl.multiple_of(row_start * GRID_W, GRID_W)

    def scores(i):
        _, _, koff = key_offset(i)
        qr = q_ref[0, i * GRID_W:(i + 1) * GRID_W, :]
        zero = jnp.zeros_like(qr)
        qs = jnp.concatenate([jnp.where(first_head, qr, zero), jnp.where(first_head, zero, qr)], axis=0)
        return _dot_nt(qs, k_ref[0, pl.ds(koff, nkeys), :])

    def finish(i):
        r, row_start, koff = key_offset(i)
        dr0 = row_start - r + (WIN_H - 1)
        bias = jnp.concatenate(
            [jnp.concatenate([bias_ref[hl, dr0 + 2 * wp] for wp in range(WIN_H // 2)], axis=1)
             for hl in range(2)], axis=0)
        s = s_ref[i % (NA_LOOKAHEAD + 1)] + bias
        e = jnp.exp(s - jnp.max(s, axis=-1, keepdims=True))
        o = _dot(e.astype(BF16), v_ref[0, pl.ds(koff, nkeys), :])
        o = o / jnp.sum(e, axis=-1, keepdims=True)
        o_ref[0, i * GRID_W:(i + 1) * GRID_W, :] = jnp.where(first_head, o[:GRID_W], o[GRID_W:])

    for i in range(NA_LOOKAHEAD):
        s_ref[i] = scores(i)
    for i in range(NA_ROWS_PER_STEP):
        ahead = i + NA_LOOKAHEAD
        if ahead < NA_ROWS_PER_STEP:
            s_ref[ahead % (NA_LOOKAHEAD + 1)] = scores(ahead)
        finish(i)


def _na(q, k, v, bias):
    b, s, _ = q.shape
    nrows = s // GRID_W
    qrows = NA_ROWS_PER_STEP * GRID_W
    return pl.pallas_call(
        functools.partial(_na_body, nrows=nrows),
        grid=(NA_HEADS // 2, b, nrows // NA_ROWS_PER_STEP),
        in_specs=[
            pl.BlockSpec((1, qrows, LANES), lambda hp, bi, rb: (bi, rb, hp)),
            pl.BlockSpec((1, s, LANES), lambda hp, bi, rb: (bi, 0, hp)),
            pl.BlockSpec((1, s, LANES), lambda hp, bi, rb: (bi, 0, hp)),
            pl.BlockSpec((2, _NA_ROW_PAIRS, GRID_W, LANES), lambda hp, bi, rb: (hp, 0, 0, 0)),
        ],
        out_specs=pl.BlockSpec((1, qrows, LANES), lambda hp, bi, rb: (bi, rb, hp)),
        out_shape=jax.ShapeDtypeStruct((b, s, NA_WIDTH), F32),
        scratch_shapes=[pltpu.VMEM((NA_LOOKAHEAD + 1, 2 * GRID_W, WIN_H * GRID_W), F32)],
        compiler_params=_params("parallel", "parallel", "arbitrary"),
        name="na",
    )(q, k, v, bias)


def _split3(x):
    x1 = x.astype(BF16)
    r1 = x - x1.astype(F32)
    x2 = r1.astype(BF16)
    x3 = (r1 - x2.astype(F32)).astype(BF16)
    return x1, x2, x3


class _GlaSubBlock:
    def __init__(self, q_ref, k_ref, v_ref, g_ref, o_ref, s_ref, b_ref, a_ref, u_ref, vrow_ref, *,
                 r0, slot, tri, a_mask, forward, exact):
        self.q_ref, self.k_ref, self.v_ref, self.g_ref = q_ref, k_ref, v_ref, g_ref
        self.o_ref, self.s_ref, self.b_ref, self.a_ref, self.u_ref = o_ref, s_ref, b_ref, a_ref, u_ref
        self.vrow_ref = vrow_ref
        self.r0, self.slot = r0, slot
        self.rows = slice(r0, r0 + GLA_SUB)
        self.tri, self.a_mask, self.forward, self.exact = tri, a_mask, forward, exact
        nchunk = GLA_SUB // GLA_CHUNK
        self.last = GLA_CHUNK - 1 if forward else 0
        self.order = tuple(range(nchunk)) if forward else tuple(reversed(range(nchunk)))

    def cumulative_decay(self):
        g = self.g_ref[0, self.rows, :]
        self.b_ref[self.slot] = sum(_dot(self.tri, part) for part in _split3(g))

    def scores(self, head_masks, head_expand):
        b = self.b_ref[self.slot]
        q = self.q_ref[0, self.rows, :]
        self.qt = (q * jnp.exp(b)).astype(BF16)
        if not self.exact:
            kt = (self.k_ref[0, self.rows, :] * jnp.exp(-b)).astype(BF16)
        self.qs = {}
        for c in range(GLA_SUB // GLA_CHUNK):
            rows = slice(c * GLA_CHUNK, (c + 1) * GLA_CHUNK)
            qc = self.qt[rows]
            qs = jnp.concatenate([jnp.where(hm, qc, jnp.zeros_like(qc)) for hm in head_masks], axis=0)
            self.qs[c] = qs
            if self.exact:
                self.a_ref[self.slot, rows, :] = self._exact_intra(c, q[rows], b[rows], head_expand)
            else:
                a = _dot_nt(qs, kt[rows])
                self.a_ref[self.slot, c] = jnp.where(self.a_mask, a, 0.0).astype(BF16)

    def _exact_intra(self, c, qc, bc, head_expand):
        base = self.r0 + c * GLA_CHUNK
        self.vrow_ref[...] = self.v_ref[0, base:base + GLA_CHUNK, :].astype(F32)
        t = lax.broadcasted_iota(jnp.int32, (GLA_CHUNK, 1), 0)

        def body(s, acc):
            krow = self.k_ref[0, pl.ds(base + s, 1), :]
            brow = self.b_ref[self.slot, pl.ds(c * GLA_CHUNK + s, 1), :]
            valid = (t >= s) if self.forward else (t < s)
            decay = jnp.exp(jnp.where(valid, bc - brow, 0.0))
            x = jnp.where(valid, qc * krow * decay, 0.0).astype(BF16)
            w = _dot(x, head_expand)
            return acc + w * self.vrow_ref[pl.ds(s, 1), :]

        return lax.fori_loop(0, GLA_CHUNK, body, jnp.zeros((GLA_CHUNK, GLA_V_WIDTH), F32))

    def increments(self, chunk_masks):
        nchunk = GLA_SUB // GLA_CHUNK
        b = self.b_ref[self.slot]
        b_last = jnp.concatenate(
            [jnp.broadcast_to(b[c * GLA_CHUNK + self.last:c * GLA_CHUNK + self.last + 1, :],
                              (GLA_CHUNK, GLA_QK_WIDTH)) for c in range(nchunk)], axis=0)
        ku_t = (self.k_ref[0, self.rows, :] * jnp.exp(b_last - b)).T.astype(BF16)
        self.decay_t = jnp.exp(b_last.T)
        for c in range(nchunk):
            pair = slice((c // 2) * 2 * GLA_CHUNK, (c // 2 + 1) * 2 * GLA_CHUNK)
            vrows = slice(self.r0 + pair.start, self.r0 + pair.stop)
            ku_pair = ku_t[:, pair]
            ku_c = jnp.where(chunk_masks[c % 2], ku_pair, jnp.zeros_like(ku_pair))
            for h in range(GLA_HEADS):
                hrows = slice(h * GLA_DK, (h + 1) * GLA_DK)
                vcols = slice(h * GLA_DV, (h + 1) * GLA_DV)
                self.u_ref[self.slot, c, hrows, :] = _dot(ku_c[hrows], self.v_ref[0, vrows, vcols])

    def advance_state(self):
        s = self.s_ref[...]
        self.states = {}
        for c in self.order:
            self.states[c] = s.astype(BF16)
            col = c * GLA_CHUNK
            s = self.decay_t[:, col:col + 1] * s + self.u_ref[self.slot, c]
        self.s_ref[...] = s

    def output_chunk(self, c):
        sub_rows = slice(c * GLA_CHUNK, (c + 1) * GLA_CHUNK)
        rows = slice(self.r0 + sub_rows.start, self.r0 + sub_rows.stop)
        inter = _dot(self.qs[c], self.states[c])
        for h in range(GLA_HEADS):
            arows = slice(h * GLA_CHUNK, (h + 1) * GLA_CHUNK)
            vcols = slice(h * GLA_DV, (h + 1) * GLA_DV)
            if self.exact:
                intra = self.a_ref[self.slot, sub_rows, vcols]
            else:
                intra = _dot(self.a_ref[self.slot, c, arows, :], self.v_ref[0, rows, vcols])
            self.o_ref[0, rows, vcols] = inter[arows] + intra


def _gla_body(exact, qf_ref, kf_ref, vf_ref, gf_ref, qb_ref, kb_ref, vb_ref, gb_ref,
              of_ref, ob_ref, sf_ref, sb_ref, bf_ref, bb_ref, af_ref, ab_ref, uf_ref, ub_ref,
              vrow_ref):
    @pl.when(pl.program_id(1) == 0)
    def _():
        sf_ref[...] = jnp.zeros_like(sf_ref)
        sb_ref[...] = jnp.zeros_like(sb_ref)

    shift = GLA_CHUNK.bit_length() - 1
    row = lax.broadcasted_iota(jnp.int32, (GLA_SUB, GLA_SUB), 0)
    col = lax.broadcasted_iota(jnp.int32, (GLA_SUB, GLA_SUB), 1)
    same_chunk = (row >> shift) == (col >> shift)
    lower = jnp.where(same_chunk & (row >= col), 1.0, 0.0).astype(BF16)
    upper = jnp.where(same_chunk & (row <= col), 1.0, 0.0).astype(BF16)
    nsub = GLA_BLOCK // GLA_SUB
    lane = lax.broadcasted_iota(jnp.int32, (GLA_CHUNK, GLA_QK_WIDTH), 1)
    head_masks = [(lane >> shift) == h for h in range(GLA_HEADS)]
    pair_lane = lax.broadcasted_iota(jnp.int32, (GLA_QK_WIDTH, 2 * GLA_CHUNK), 1)
    chunk_masks = [(pair_lane >> shift) == j for j in range(2)]
    t = lax.broadcasted_iota(jnp.int32, (GLA_HEADS * GLA_CHUNK, GLA_CHUNK), 0) & (GLA_CHUNK - 1)
    s = lax.broadcasted_iota(jnp.int32, (GLA_HEADS * GLA_CHUNK, GLA_CHUNK), 1)
    head_expand = None
    if exact:
        erow = lax.broadcasted_iota(jnp.int32, (GLA_QK_WIDTH, GLA_V_WIDTH), 0) // GLA_DK
        ecol = lax.broadcasted_iota(jnp.int32, (GLA_QK_WIDTH, GLA_V_WIDTH), 1) // GLA_DV
        head_expand = jnp.where(erow == ecol, 1.0, 0.0).astype(BF16)
    fwd = [_GlaSubBlock(qf_ref, kf_ref, vf_ref, gf_ref, of_ref, sf_ref, bf_ref, af_ref, uf_ref,
                        vrow_ref, r0=i * GLA_SUB, slot=i, tri=lower, a_mask=t >= s, forward=True,
                        exact=exact) for i in range(nsub)]
    bwd = [_GlaSubBlock(qb_ref, kb_ref, vb_ref, gb_ref, ob_ref, sb_ref, bb_ref, ab_ref, ub_ref,
                        vrow_ref, r0=i * GLA_SUB, slot=i, tri=upper, a_mask=t < s, forward=False,
                        exact=exact) for i in reversed(range(nsub))]
    both = [blk for pair in zip(fwd, bwd) for blk in pair]
    for blk in both:
        blk.cumulative_decay()
    for blk in both:
        blk.scores(head_masks, head_expand)
    for blk in both:
        blk.increments(chunk_masks)
    for blk in both:
        blk.advance_state()
    for f, b in zip(fwd, bwd):
        for cf, cb in zip(f.order, b.order):
            f.output_chunk(cf)
            b.output_chunk(cb)


def _gla(q, k, v, lg, *, exact):
    b, s, _ = q.shape
    nb = s // GLA_BLOCK

    def fwd(width, j=0):
        return pl.BlockSpec((1, GLA_BLOCK, width), lambda bi, n: (bi, n, j))

    def bwd(width, j=0):
        return pl.BlockSpec((1, GLA_BLOCK, width), lambda bi, n: (bi, nb - 1 - n, j))

    nsub = GLA_BLOCK // GLA_SUB
    nchunk = GLA_SUB // GLA_CHUNK
    state = pltpu.VMEM((GLA_QK_WIDTH, GLA_DV), F32)
    decay = pltpu.VMEM((nsub, GLA_SUB, GLA_QK_WIDTH), F32)
    if exact:
        intra = pltpu.VMEM((nsub, GLA_SUB, GLA_V_WIDTH), F32)
    else:
        intra = pltpu.VMEM((nsub, nchunk, GLA_HEADS * GLA_CHUNK, GLA_CHUNK), BF16)
    increments = pltpu.VMEM((nsub, nchunk, GLA_QK_WIDTH, GLA_DV), F32)
    vrows = pltpu.VMEM((GLA_CHUNK, GLA_V_WIDTH), F32)
    return pl.pallas_call(
        functools.partial(_gla_body, exact),
        grid=(b, nb),
        in_specs=[fwd(GLA_QK_WIDTH), fwd(GLA_QK_WIDTH), fwd(GLA_V_WIDTH), fwd(GLA_QK_WIDTH, 0),
                  bwd(GLA_QK_WIDTH), bwd(GLA_QK_WIDTH), bwd(GLA_V_WIDTH), bwd(GLA_QK_WIDTH, 1)],
        out_specs=[fwd(GLA_V_WIDTH), bwd(GLA_V_WIDTH)],
        out_shape=[jax.ShapeDtypeStruct((b, s, GLA_V_WIDTH), F32)] * 2,
        scratch_shapes=[state, state, decay, decay, intra, intra, increments, increments, vrows],
        compiler_params=_params("parallel", "arbitrary"),
        name="gla_exact" if exact else "gla",
    )(q, k, v, lg, q, k, v, lg)


def _mix_update(x, rows, na_ref, nag_ref, of_ref, ob_ref, gg_ref, gr_ref, wo_ref):
    na = _rmsnorm(na_ref[rows, :], nag_ref[...]).astype(BF16)
    y = _dot(na, wo_ref[:NA_WIDTH, :])
    o = of_ref[rows, :] + ob_ref[rows, :]
    r = gr_ref[rows, :]
    gate = r * jax.nn.sigmoid(r)
    heads = []
    for h in range(GLA_HEADS):
        cols = slice(h * GLA_DV, (h + 1) * GLA_DV)
        heads.append((_rmsnorm(o[:, cols], gg_ref[...]) * gate[:, cols]).astype(BF16))
    return x + y + _dot(jnp.concatenate(heads, axis=-1), wo_ref[NA_WIDTH:, :])


def _mix_ffn_body(final_norm, x_ref, na_ref, nag_ref, of_ref, ob_ref, gg_ref, gr_ref, woc_ref,
                  g_ref, wgc_ref, wuc_ref, wdc_ref, fg_ref, o_ref, wo_ref, wg_ref, wu_ref, wd_ref):
    step = pl.program_id(0)
    nstage = wg_ref.shape[1] // CAST_CHUNK
    _stage_weight_chunks(step, ((woc_ref, wo_ref, 1, wo_ref.shape[1] // CAST_CHUNK),
                                (wgc_ref, wg_ref, 1, nstage), (wuc_ref, wu_ref, 1, nstage),
                                (wdc_ref, wd_ref, 0, nstage)))

    @pl.when(step >= nstage)
    def _():
        subtiles = [slice(r, r + TOKEN_SUBTILE) for r in range(0, TM_TOKENS, TOKEN_SUBTILE)]
        for rows in subtiles:
            o_ref[rows, :] = _mix_update(x_ref[rows, :], rows, na_ref, nag_ref, of_ref, ob_ref,
                                         gg_ref, gr_ref, wo_ref)
        for rows in subtiles:
            y = _swiglu_update(o_ref[rows, :], g_ref, wg_ref, wu_ref, wd_ref)
            o_ref[rows, :] = _rmsnorm(y, fg_ref[...]) if final_norm else y


def _mix_ffn(x, na_o, na_gain, o_f, o_b, gla_gain, g_r, w_out, norm_g, wg, wu, wd, layer,
             final_g, final_norm):
    n, d = x.shape
    d_ff = wg.shape[2]
    tm = TM_TOKENS
    nstage = d_ff // CAST_CHUNK

    def tile(width):
        return _token_tile(tm, width, nstage)

    return pl.pallas_call(
        functools.partial(_mix_ffn_body, final_norm),
        grid=(nstage + n // tm,),
        in_specs=[tile(d), tile(NA_WIDTH), _const_spec((1, NA_WIDTH)), tile(GLA_V_WIDTH),
                  tile(GLA_V_WIDTH), _const_spec((1, GLA_DV)), tile(GLA_V_WIDTH),
                  _col_chunk_spec(layer, w_out.shape[1], d // CAST_CHUNK),
                  _const_spec((1, d)), _col_chunk_spec(layer, d, nstage),
                  _col_chunk_spec(layer, d, nstage), _row_chunk_spec(layer, d, nstage),
                  _const_spec((1, d))],
        out_specs=tile(d),
        out_shape=jax.ShapeDtypeStruct((n, d), F32),
        scratch_shapes=[pltpu.VMEM(w_out.shape[1:], BF16), pltpu.VMEM((d, d_ff), BF16),
                        pltpu.VMEM((d, d_ff), BF16), pltpu.VMEM((d_ff, d), BF16)],
        compiler_params=_params("arbitrary"),
        name="mix_ffn_final" if final_norm else "mix_ffn",
    )(x, na_o, na_gain, o_f, o_b, gla_gain, g_r, w_out, norm_g, wg, wu, wd, final_g)


def _gate_weights(w_f, b_f, w_b, b_b):
    w = jnp.zeros((CODE_PAD, 2 * GLA_QK_WIDTH), F32)
    w = w.at[:GLA_GATE_RANK, :GLA_QK_WIDTH].set(w_f)
    w = w.at[GLA_GATE_RANK:2 * GLA_GATE_RANK, GLA_QK_WIDTH:].set(w_b)
    return w, jnp.concatenate([b_f, b_b])[None, :]


def kernel(x, ffn1_norm, ffn1_wg, ffn1_wu, ffn1_wd, mix_norm, w_in, na_rpb, na_gain, w_gate_f, b_gate_f, w_gate_b, b_gate_b, gla_gain, w_out, ffn2_norm, ffn2_wg, ffn2_wu, ffn2_wd, final_norm):
    bsz, seq, d = x.shape
    depth = ffn1_norm.shape[0]
    n = bsz * seq
    xt = x.reshape(n, d)
    bias = _na_bias_tables(na_rpb)
    w_in_t = jnp.swapaxes(w_in, 1, 2)
    for l in range(depth):
        w_code = jnp.pad(w_in_t[l, MAIN_WIDTH:, :].T, ((0, 0), (0, CODE_PAD - 2 * GLA_GATE_RANK))).astype(BF16)
        w_gate, b_gate = _gate_weights(w_gate_f[l], b_gate_f[l], w_gate_b[l], b_gate_b[l])
        xt, na_q, na_k, na_v, g_q, g_k, g_v, g_r, lg, min_decay = _ffn_inproj(
            xt, ffn1_norm[l][None], ffn1_wg, ffn1_wu, ffn1_wd, mix_norm[l][None], w_in_t, w_code,
            w_gate, b_gate, l)
        to3 = lambda a: a.reshape(bsz, seq, a.shape[-1])
        na_o = _na(to3(na_q), to3(na_k), to3(na_v), bias[l])
        o_f, o_b = lax.cond(jnp.min(min_decay) >= -GLA_SAFE_DECAY,
                            functools.partial(_gla, exact=False), functools.partial(_gla, exact=True),
                            to3(g_q), to3(g_k), to3(g_v), to3(lg))
        xt = _mix_ffn(xt, na_o.reshape(n, NA_WIDTH), na_gain[l][None], o_f.reshape(n, GLA_V_WIDTH),
                      o_b.reshape(n, GLA_V_WIDTH), gla_gain[l][None], g_r, w_out, ffn2_norm[l][None],
                      ffn2_wg, ffn2_wu, ffn2_wd, l, final_norm[None], final_norm=l == depth - 1)
    return xt.reshape(bsz, seq, d)
```

```python
import functools

import numpy as np
import jax
import jax.numpy as jnp
from jax import lax
from jax.experimental import pallas as pl
from jax.experimental.pallas import tpu as pltpu

F32 = jnp.float32
BF16 = jnp.bfloat16

EPS = 1e-6
NEG_INF = -1e30

GRID_W = 64
WIN_H = 8
WIN_W = 16
NA_HEADS = 8
NA_HEAD_DIM = 64
NA_WIDTH = NA_HEADS * NA_HEAD_DIM
GLA_HEADS = 4
GLA_DK = 64
GLA_DV = 128
GLA_QK_WIDTH = GLA_HEADS * GLA_DK
GLA_V_WIDTH = GLA_HEADS * GLA_DV
GLA_GATE_RANK = 16
GLA_TAU = 16.0
GLA_CHUNK = 64
MAIN_WIDTH = 3 * NA_WIDTH + 2 * GLA_QK_WIDTH + 2 * GLA_V_WIDTH

LANES = 128
SUBLANES = 8
GLA_SAFE_DECAY = 60.0
CODE_PAD = LANES
VMEM_LIMIT = 56 * 1024 * 1024

TM_TOKENS = 512
TOKEN_SUBTILE = 256
PROJ_DOT_WIDTH = 1024
FFN_DOT_WIDTH = 1024
CAST_CHUNK = 256
NA_ROWS_PER_STEP = 64
NA_LOOKAHEAD = 3
GLA_BLOCK = 1024
GLA_SUB = 256


def _rmsnorm(x, g):
    return x * lax.rsqrt(jnp.mean(x * x, axis=-1, keepdims=True) + EPS) * g


def _dot(a, b):
    return jnp.dot(a, b, preferred_element_type=F32)


def _dot_nt(a, b):
    return lax.dot_general(a, b, (((1,), (1,)), ((), ())), preferred_element_type=F32)


def _const_spec(shape):
    return pl.BlockSpec(shape, lambda *_: (0,) * len(shape), pipeline_mode=pl.Buffered(1))


def _params(*sem):
    return pltpu.CompilerParams(dimension_semantics=sem, vmem_limit_bytes=VMEM_LIMIT)


def _ffn_chunks(d_ff):
    chunks, start = [], 0
    while start < d_ff:
        width = min(FFN_DOT_WIDTH, d_ff - start)
        chunks.append((start, width))
        start += width
    return tuple(chunks)


def _swiglu_update(x, g_ref, wg_ref, wu_ref, wd_ref):
    xb = _rmsnorm(x, g_ref[...]).astype(BF16)
    y = None
    for start, width in _ffn_chunks(wg_ref.shape[1]):
        h = _dot(xb, wg_ref[:, start:start + width])
        u = _dot(xb, wu_ref[:, start:start + width])
        a = (h * jax.nn.sigmoid(h) * u).astype(BF16)
        part = _dot(a, wd_ref[start:start + width, :])
        y = part if y is None else y + part
    return x + 0.5 * y


def _stage_weight_chunks(step, stages):
    for chunk_ref, scratch_ref, axis, nchunks in stages:
        for c in range(nchunks):
            @pl.when(step == c)
            def _(chunk_ref=chunk_ref, scratch_ref=scratch_ref, axis=axis, c=c):
                span = slice(c * CAST_CHUNK, (c + 1) * CAST_CHUNK)
                if axis == 0:
                    scratch_ref[span, :] = chunk_ref[...].astype(BF16)
                elif axis == 1:
                    scratch_ref[:, span] = chunk_ref[...].astype(BF16)
                else:
                    scratch_ref[:, span] = chunk_ref[...].T.astype(BF16)


def _col_chunk_spec(layer, rows, nchunks):
    return pl.BlockSpec((None, rows, CAST_CHUNK), lambda i: (layer, 0, jnp.minimum(i, nchunks - 1)))


def _row_chunk_spec(layer, cols, nchunks):
    return pl.BlockSpec((None, CAST_CHUNK, cols), lambda i: (layer, jnp.minimum(i, nchunks - 1), 0))


def _token_tile(tm, width, nstage):
    return pl.BlockSpec((tm, width), lambda i: (jnp.maximum(i - nstage, 0), 0))


def _log_sigmoid(x):
    return jnp.minimum(x, 0.0) - jnp.log(1.0 + jnp.exp(-jnp.abs(x)))


def _project_rows(x, rows, g_ref, wm_ref, wc_ref, wgate, bgate_ref, outs, lg_ref):
    xb = _rmsnorm(x, g_ref[...]).astype(BF16)
    codes = _dot(xb, wc_ref[...])
    logits = _dot(codes.astype(BF16), wgate) + bgate_ref[...]
    lg = _log_sigmoid(logits) * (1.0 / GLA_TAU)
    lg_ref[rows, :] = lg
    chunk_sums = [jnp.sum(lg[c:c + GLA_CHUNK], axis=0, keepdims=True)
                  for c in range(0, lg.shape[0], GLA_CHUNK)]
    min_decay = jnp.min(functools.reduce(jnp.minimum, chunk_sums), axis=1, keepdims=True)
    col = 0
    pending = list(outs)
    while pending:
        group, width = [], 0
        while pending and width + pending[0][1] <= PROJ_DOT_WIDTH:
            group.append(pending.pop(0))
            width += group[-1][1]
        p = _dot(xb, wm_ref[:, col:col + width])
        col += width
        off = 0
        for ref, w, scale in group:
            piece = p[:, off:off + w]
            if scale is not None:
                piece = piece * scale
            ref[rows, :] = piece.astype(ref.dtype)
            off += w
    return min_decay


def _ffn_inproj_body(x_ref, g1_ref, wgc_ref, wuc_ref, wdc_ref, g2_ref, wmc_ref, wc_ref,
                     wgate_ref, bgate_ref,
                     xo_ref, naq_ref, nak_ref, nav_ref, gq_ref, gk_ref, gv_ref, gr_ref, lg_ref,
                     md_ref, wg_ref, wu_ref, wd_ref, wm_ref, wcs_ref):
    step = pl.program_id(0)
    nffn = wg_ref.shape[1] // CAST_CHUNK
    nproj = wm_ref.shape[1] // CAST_CHUNK
    _stage_weight_chunks(step, ((wgc_ref, wg_ref, 1, nffn), (wuc_ref, wu_ref, 1, nffn),
                                (wdc_ref, wd_ref, 0, nffn), (wmc_ref, wm_ref, "1t", nproj)))

    @pl.when(step == 0)
    def _():
        wcs_ref[...] = wc_ref[...].T.astype(BF16)

    @pl.when(step >= max(nffn, nproj))
    def _():
        outs = ((naq_ref, NA_WIDTH, NA_HEAD_DIM ** -0.5), (nak_ref, NA_WIDTH, None),
                (nav_ref, NA_WIDTH, None), (gq_ref, GLA_QK_WIDTH, GLA_DK ** -0.5),
                (gk_ref, GLA_QK_WIDTH, None), (gv_ref, GLA_V_WIDTH, None),
                (gr_ref, GLA_V_WIDTH, None))
        wgate = wgate_ref[...].astype(BF16)
        subtiles = [slice(r, r + TOKEN_SUBTILE) for r in range(0, TM_TOKENS, TOKEN_SUBTILE)]
        for rows in subtiles:
            xo_ref[rows, :] = _swiglu_update(x_ref[rows, :], g1_ref, wg_ref, wu_ref, wd_ref)
        mins = [_project_rows(xo_ref[rows, :], rows, g2_ref, wm_ref, wcs_ref, wgate, bgate_ref,
                              outs, lg_ref) for rows in subtiles]
        md_ref[...] = jnp.broadcast_to(functools.reduce(jnp.minimum, mins), md_ref.shape)


def _ffn_inproj(x, ffn_g, wg, wu, wd, mix_g, w_in_t, w_code, w_gate, b_gate, layer):
    n, d = x.shape
    d_ff = wg.shape[2]
    nffn = d_ff // CAST_CHUNK
    nproj = MAIN_WIDTH // CAST_CHUNK
    nstage = max(nffn, nproj)

    def tile(width):
        return _token_tile(TM_TOKENS, width, nstage)

    widths = (d, NA_WIDTH, NA_WIDTH, NA_WIDTH, GLA_QK_WIDTH, GLA_QK_WIDTH, GLA_V_WIDTH,
              GLA_V_WIDTH, 2 * GLA_QK_WIDTH)
    dtypes = (F32, BF16, BF16, BF16, F32, F32, BF16, F32, F32)
    return pl.pallas_call(
        _ffn_inproj_body,
        grid=(nstage + n // TM_TOKENS,),
        in_specs=[tile(d), _const_spec((1, d)), _col_chunk_spec(layer, d, nffn),
                  _col_chunk_spec(layer, d, nffn), _row_chunk_spec(layer, d, nffn),
                  _const_spec((1, d)), _row_chunk_spec(layer, d, nproj),
                  _const_spec(w_code.shape), _const_spec(w_gate.shape), _const_spec(b_gate.shape)],
        out_specs=[tile(w) for w in widths] + [_token_tile(SUBLANES, LANES, nstage)],
        out_shape=[jax.ShapeDtypeStruct((n, w), dt) for w, dt in zip(widths, dtypes)]
        + [jax.ShapeDtypeStruct((n // TM_TOKENS * SUBLANES, LANES), F32)],
        scratch_shapes=[pltpu.VMEM((d, d_ff), BF16), pltpu.VMEM((d, d_ff), BF16),
                        pltpu.VMEM((d_ff, d), BF16), pltpu.VMEM((d, MAIN_WIDTH), BF16),
                        pltpu.VMEM((d, CODE_PAD), BF16)],
        compiler_params=_params("arbitrary"),
        name="ffn_inproj",
    )(x, ffn_g, wg, wu, wd, mix_g, w_in_t, w_code, w_gate, b_gate)


_NA_ROW_PAIRS = 2 * WIN_H - 2


def _na_bias_tables(rpb):
    ncol = 2 * WIN_W - 1
    qc = np.arange(GRID_W)
    kc = np.arange(GRID_W)
    col_start = np.clip(qc - WIN_W // 2, 0, GRID_W - WIN_W)
    valid = (kc[None, :] >= col_start[:, None]) & (kc[None, :] < col_start[:, None] + WIN_W)
    dc = np.clip(kc[None, :] - qc[:, None] + (WIN_W - 1), 0, ncol - 1)
    onehot = (np.arange(ncol)[:, None, None] == dc[None]).astype(np.float32)
    selector = np.einsum("ab,cqk->acqbk", np.eye(2, dtype=np.float32), onehot)
    selector = selector.reshape(2 * ncol, GRID_W * 2 * GRID_W)
    pairs = jnp.concatenate([rpb[:, :, :-1], rpb[:, :, 1:]], axis=-1).astype(F32)
    lead = pairs.shape[:-1]
    t = jnp.dot(pairs.reshape(-1, 2 * ncol), selector, precision=lax.Precision.HIGHEST)
    t = t.reshape(*lead, GRID_W, 2 * GRID_W)
    return jnp.where(np.concatenate([valid, valid], axis=-1), t, NEG_INF)


def _na_body(q_ref, k_ref, v_ref, bias_ref, o_ref, s_ref, *, nrows):
    rb = pl.program_id(2)
    half = WIN_H // 2
    lane = lax.broadcasted_iota(jnp.int32, (GRID_W, LANES), 1)
    first_head = lane < NA_HEAD_DIM
    nkeys = WIN_H * GRID_W

    def key_offset(i):
        r = rb * NA_ROWS_PER_STEP + i
        row_start = jnp.clip(r - half, 0, nrows - WIN_H)
        return r, row_start, pl.multiple_of(row_start * GRID_W, GRID_W)

    def scores(i):
        _, _, koff = key_offset(i)
        qr = q_ref[0, i * GRID_W:(i + 1) * GRID_W, :]
        zero = jnp.zeros_like(qr)
        qs = jnp.concatenate([jnp.where(first_head, qr, zero), jnp.where(first_head, zero, qr)], axis=0)
        return _dot_nt(qs, k_ref[0, pl.ds(koff, nkeys), :])

    def finish(i):
        r, row_start, koff = key_offset(i)
        dr0 = row_start - r + (WIN_H - 1)
        bias = jnp.concatenate(
            [jnp.concatenate([bias_ref[hl, dr0 + 2 * wp] for wp in range(WIN_H // 2)], axis=1)
             for hl in range(2)], axis=0)
        s = s_ref[i % (NA_LOOKAHEAD + 1)] + bias
        e = jnp.exp(s - jnp.max(s, axis=-1, keepdims=True))
        o = _dot(e.astype(BF16), v_ref[0, pl.ds(koff, nkeys), :])
        o = o / jnp.sum(e, axis=-1, keepdims=True)
        o_ref[0, i * GRID_W:(i + 1) * GRID_W, :] = jnp.where(first_head, o[:GRID_W], o[GRID_W:])

    for i in range(NA_LOOKAHEAD):
        s_ref[i] = scores(i)
    for i in range(NA_ROWS_PER_STEP):
        ahead = i + NA_LOOKAHEAD
        if ahead < NA_ROWS_PER_STEP:
            s_ref[ahead % (NA_LOOKAHEAD + 1)] = scores(ahead)
        finish(i)


def _na(q, k, v, bias):
    b, s, _ = q.shape
    nrows = s // GRID_W
    qrows = NA_ROWS_PER_STEP * GRID_W
    return pl.pallas_call(
        functools.partial(_na_body, nrows=nrows),
        grid=(NA_HEADS // 2, b, nrows // NA_ROWS_PER_STEP),
        in_specs=[
            pl.BlockSpec((1, qrows, LANES), lambda hp, bi, rb: (bi, rb, hp)),
            pl.BlockSpec((1, s, LANES), lambda hp, bi, rb: (bi, 0, hp)),
            pl.BlockSpec((1, s, LANES), lambda hp, bi, rb: (bi, 0, hp)),
            pl.BlockSpec((2, _NA_ROW_PAIRS, GRID_W, LANES), lambda hp, bi, rb: (hp, 0, 0, 0)),
        ],
        out_specs=pl.BlockSpec((1, qrows, LANES), lambda hp, bi, rb: (bi, rb, hp)),
        out_shape=jax.ShapeDtypeStruct((b, s, NA_WIDTH), F32),
        scratch_shapes=[pltpu.VMEM((NA_LOOKAHEAD + 1, 2 * GRID_W, WIN_H * GRID_W), F32)],
        compiler_params=_params("parallel", "parallel", "arbitrary"),
        name="na",
    )(q, k, v, bias)


def _split3(x):
    x1 = x.astype(BF16)
    r1 = x - x1.astype(F32)
    x2 = r1.astype(BF16)
    x3 = (r1 - x2.astype(F32)).astype(BF16)
    return x1, x2, x3


class _GlaSubBlock:
    def __init__(self, q_ref, k_ref, v_ref, g_ref, o_ref, s_ref, b_ref, a_ref, u_ref, vrow_ref, *,
                 r0, slot, tri, a_mask, forward, exact):
        self.q_ref, self.k_ref, self.v_ref, self.g_ref = q_ref, k_ref, v_ref, g_ref
        self.o_ref, self.s_ref, self.b_ref, self.a_ref, self.u_ref = o_ref, s_ref, b_ref, a_ref, u_ref
        self.vrow_ref = vrow_ref
        self.r0, self.slot = r0, slot
        self.rows = slice(r0, r0 + GLA_SUB)
        self.tri, self.a_mask, self.forward, self.exact = tri, a_mask, forward, exact
        nchunk = GLA_SUB // GLA_CHUNK
        self.last = GLA_CHUNK - 1 if forward else 0
        self.order = tuple(range(nchunk)) if forward else tuple(reversed(range(nchunk)))

    def cumulative_decay(self):
        g = self.g_ref[0, self.rows, :]
        self.b_ref[self.slot] = sum(_dot(self.tri, part) for part in _split3(g))

    def scores(self, head_masks, head_expand):
        b = self.b_ref[self.slot]
        q = self.q_ref[0, self.rows, :]
        self.qt = (q * jnp.exp(b)).astype(BF16)
        if not self.exact:
            kt = (self.k_ref[0, self.rows, :] * jnp.exp(-b)).astype(BF16)
        self.qs = {}
        for c in range(GLA_SUB // GLA_CHUNK):
            rows = slice(c * GLA_CHUNK, (c + 1) * GLA_CHUNK)
            qc = self.qt[rows]
            qs = jnp.concatenate([jnp.where(hm, qc, jnp.zeros_like(qc)) for hm in head_masks], axis=0)
            self.qs[c] = qs
            if self.exact:
                self.a_ref[self.slot, rows, :] = self._exact_intra(c, q[rows], b[rows], head_expand)
            else:
                a = _dot_nt(qs, kt[rows])
                self.a_ref[self.slot, c] = jnp.where(self.a_mask, a, 0.0).astype(BF16)

    def _exact_intra(self, c, qc, bc, head_expand):
        base = self.r0 + c * GLA_CHUNK
        self.vrow_ref[...] = self.v_ref[0, base:base + GLA_CHUNK, :].astype(F32)
        t = lax.broadcasted_iota(jnp.int32, (GLA_CHUNK, 1), 0)

        def body(s, acc):
            krow = self.k_ref[0, pl.ds(base + s, 1), :]
            brow = self.b_ref[self.slot, pl.ds(c * GLA_CHUNK + s, 1), :]
            valid = (t >= s) if self.forward else (t < s)
            decay = jnp.exp(jnp.where(valid, bc - brow, 0.0))
            x = jnp.where(valid, qc * krow * decay, 0.0).astype(BF16)
            w = _dot(x, head_expand)
            return acc + w * self.vrow_ref[pl.ds(s, 1), :]

        return lax.fori_loop(0, GLA_CHUNK, body, jnp.zeros((GLA_CHUNK, GLA_V_WIDTH), F32))

    def increments(self, chunk_masks):
        nchunk = GLA_SUB // GLA_CHUNK
        b = self.b_ref[self.slot]
        b_last = jnp.concatenate(
            [jnp.broadcast_to(b[c * GLA_CHUNK + self.last:c * GLA_CHUNK + self.last + 1, :],
                              (GLA_CHUNK, GLA_QK_WIDTH)) for c in range(nchunk)], axis=0)
        ku_t = (self.k_ref[0, self.rows, :] * jnp.exp(b_last - b)).T.astype(BF16)
        self.decay_t = jnp.exp(b_last.T)
        for c in range(nchunk):
            pair = slice((c // 2) * 2 * GLA_CHUNK, (c // 2 + 1) * 2 * GLA_CHUNK)
            vrows = slice(self.r0 + pair.start, self.r0 + pair.stop)
            ku_pair = ku_t[:, pair]
            ku_c = jnp.where(chunk_masks[c % 2], ku_pair, jnp.zeros_like(ku_pair))
            for h in range(GLA_HEADS):
                hrows = slice(h * GLA_DK, (h + 1) * GLA_DK)
                vcols = slice(h * GLA_DV, (h + 1) * GLA_DV)
                self.u_ref[self.slot, c, hrows, :] = _dot(ku_c[hrows], self.v_ref[0, vrows, vcols])

    def advance_state(self):
        s = self.s_ref[...]
        self.states = {}
        for c in self.order:
            self.states[c] = s.astype(BF16)
            col = c * GLA_CHUNK
            s = self.decay_t[:, col:col + 1] * s + self.u_ref[self.slot, c]
        self.s_ref[...] = s

    def output_chunk(self, c):
        sub_rows = slice(c * GLA_CHUNK, (c + 1) * GLA_CHUNK)
        rows = slice(self.r0 + sub_rows.start, self.r0 + sub_rows.stop)
        inter = _dot(self.qs[c], self.states[c])
        for h in range(GLA_HEADS):
            arows = slice(h * GLA_CHUNK, (h + 1) * GLA_CHUNK)
            vcols = slice(h * GLA_DV, (h + 1) * GLA_DV)
            if self.exact:
                intra = self.a_ref[self.slot, sub_rows, vcols]
            else:
                intra = _dot(self.a_ref[self.slot, c, arows, :], self.v_ref[0, rows, vcols])
            self.o_ref[0, rows, vcols] = inter[arows] + intra


def _gla_body(exact, qf_ref, kf_ref, vf_ref, gf_ref, qb_ref, kb_ref, vb_ref, gb_ref,
              of_ref, ob_ref, sf_ref, sb_ref, bf_ref, bb_ref, af_ref, ab_ref, uf_ref, ub_ref,
              vrow_ref):
    @pl.when(pl.program_id(1) == 0)
    def _():
        sf_ref[...] = jnp.zeros_like(sf_ref)
        sb_ref[...] = jnp.zeros_like(sb_ref)

    shift = GLA_CHUNK.bit_length() - 1
    row = lax.broadcasted_iota(jnp.int32, (GLA_SUB, GLA_SUB), 0)
    col = lax.broadcasted_iota(jnp.int32, (GLA_SUB, GLA_SUB), 1)
    same_chunk = (row >> shift) == (col >> shift)
    lower = jnp.where(same_chunk & (row >= col), 1.0, 0.0).astype(BF16)
    upper = jnp.where(same_chunk & (row <= col), 1.0, 0.0).astype(BF16)
    nsub = GLA_BLOCK // GLA_SUB
    lane = lax.broadcasted_iota(jnp.int32, (GLA_CHUNK, GLA_QK_WIDTH), 1)
    head_masks = [(lane >> shift) == h for h in range(GLA_HEADS)]
    pair_lane = lax.broadcasted_iota(jnp.int32, (GLA_QK_WIDTH, 2 * GLA_CHUNK), 1)
    chunk_masks = [(pair_lane >> shift) == j for j in range(2)]
    t = lax.broadcasted_iota(jnp.int32, (GLA_HEADS * GLA_CHUNK, GLA_CHUNK), 0) & (GLA_CHUNK - 1)
    s = lax.broadcasted_iota(jnp.int32, (GLA_HEADS * GLA_CHUNK, GLA_CHUNK), 1)
    head_expand = None
    if exact:
        erow = lax.broadcasted_iota(jnp.int32, (GLA_QK_WIDTH, GLA_V_WIDTH), 0) // GLA_DK
        ecol = lax.broadcasted_iota(jnp.int32, (GLA_QK_WIDTH, GLA_V_WIDTH), 1) // GLA_DV
        head_expand = jnp.where(erow == ecol, 1.0, 0.0).astype(BF16)
    fwd = [_GlaSubBlock(qf_ref, kf_ref, vf_ref, gf_ref, of_ref, sf_ref, bf_ref, af_ref, uf_ref,
                        vrow_ref, r0=i * GLA_SUB, slot=i, tri=lower, a_mask=t >= s, forward=True,
                        exact=exact) for i in range(nsub)]
    bwd = [_GlaSubBlock(qb_ref, kb_ref, vb_ref, gb_ref, ob_ref, sb_ref, bb_ref, ab_ref, ub_ref,
                        vrow_ref, r0=i * GLA_SUB, slot=i, tri=upper, a_mask=t < s, forward=False,
                        exact=exact) for i in reversed(range(nsub))]
    both = [blk for pair in zip(fwd, bwd) for blk in pair]
    for blk in both:
        blk.cumulative_decay()
    for blk in both:
        blk.scores(head_masks, head_expand)
    for blk in both:
        blk.increments(chunk_masks)
    for blk in both:
        blk.advance_state()
    for f, b in zip(fwd, bwd):
        for cf, cb in zip(f.order, b.order):
            f.output_chunk(cf)
            b.output_chunk(cb)


def _gla(q, k, v, lg, *, exact):
    b, s, _ = q.shape
    nb = s // GLA_BLOCK

    def fwd(width, j=0):
        return pl.BlockSpec((1, GLA_BLOCK, width), lambda bi, n: (bi, n, j))

    def bwd(width, j=0):
        return pl.BlockSpec((1, GLA_BLOCK, width), lambda bi, n: (bi, nb - 1 - n, j))

    nsub = GLA_BLOCK // GLA_SUB
    nchunk = GLA_SUB // GLA_CHUNK
    state = pltpu.VMEM((GLA_QK_WIDTH, GLA_DV), F32)
    decay = pltpu.VMEM((nsub, GLA_SUB, GLA_QK_WIDTH), F32)
    if exact:
        intra = pltpu.VMEM((nsub, GLA_SUB, GLA_V_WIDTH), F32)
    else:
        intra = pltpu.VMEM((nsub, nchunk, GLA_HEADS * GLA_CHUNK, GLA_CHUNK), BF16)
    increments = pltpu.VMEM((nsub, nchunk, GLA_QK_WIDTH, GLA_DV), F32)
    vrows = pltpu.VMEM((GLA_CHUNK, GLA_V_WIDTH), F32)
    return pl.pallas_call(
        functools.partial(_gla_body, exact),
        grid=(b, nb),
        in_specs=[fwd(GLA_QK_WIDTH), fwd(GLA_QK_WIDTH), fwd(GLA_V_WIDTH), fwd(GLA_QK_WIDTH, 0),
                  bwd(GLA_QK_WIDTH), bwd(GLA_QK_WIDTH), bwd(GLA_V_WIDTH), bwd(GLA_QK_WIDTH, 1)],
        out_specs=[fwd(GLA_V_WIDTH), bwd(GLA_V_WIDTH)],
        out_shape=[jax.ShapeDtypeStruct((b, s, GLA_V_WIDTH), F32)] * 2,
        scratch_shapes=[state, state, decay, decay, intra, intra, increments, increments, vrows],
        compiler_params=_params("parallel", "arbitrary"),
        name="gla_exact" if exact else "gla",
    )(q, k, v, lg, q, k, v, lg)


def _mix_update(x, rows, na_ref, nag_ref, of_ref, ob_ref, gg_ref, gr_ref, wo_ref):
    na = _rmsnorm(na_ref[rows, :], nag_ref[...]).astype(BF16)
    y = _dot(na, wo_ref[:NA_WIDTH, :])
    o = of_ref[rows, :] + ob_ref[rows, :]
    r = gr_ref[rows, :]
    gate = r * jax.nn.sigmoid(r)
    heads = []
    for h in range(GLA_HEADS):
        cols = slice(h * GLA_DV, (h + 1) * GLA_DV)
        heads.append((_rmsnorm(o[:, cols], gg_ref[...]) * gate[:, cols]).astype(BF16))
    return x + y + _dot(jnp.concatenate(heads, axis=-1), wo_ref[NA_WIDTH:, :])


def _mix_ffn_body(final_norm, x_ref, na_ref, nag_ref, of_ref, ob_ref, gg_ref, gr_ref, woc_ref,
                  g_ref, wgc_ref, wuc_ref, wdc_ref, fg_ref, o_ref, wo_ref, wg_ref, wu_ref, wd_ref):
    step = pl.program_id(0)
    nstage = wg_ref.shape[1] // CAST_CHUNK
    _stage_weight_chunks(step, ((woc_ref, wo_ref, 1, wo_ref.shape[1] // CAST_CHUNK),
                                (wgc_ref, wg_ref, 1, nstage), (wuc_ref, wu_ref, 1, nstage),
                                (wdc_ref, wd_ref, 0, nstage)))

    @pl.when(step >= nstage)
    def _():
        subtiles = [slice(r, r + TOKEN_SUBTILE) for r in range(0, TM_TOKENS, TOKEN_SUBTILE)]
        for rows in subtiles:
            o_ref[rows, :] = _mix_update(x_ref[rows, :], rows, na_ref, nag_ref, of_ref, ob_ref,
                                         gg_ref, gr_ref, wo_ref)
        for rows in subtiles:
            y = _swiglu_update(o_ref[rows, :], g_ref, wg_ref, wu_ref, wd_ref)
            o_ref[rows, :] = _rmsnorm(y, fg_ref[...]) if final_norm else y


def _mix_ffn(x, na_o, na_gain, o_f, o_b, gla_gain, g_r, w_out, norm_g, wg, wu, wd, layer,
             final_g, final_norm):
    n, d = x.shape
    d_ff = wg.shape[2]
    tm = TM_TOKENS
    nstage = d_ff // CAST_CHUNK

    def tile(width):
        return _token_tile(tm, width, nstage)

    return pl.pallas_call(
        functools.partial(_mix_ffn_body, final_norm),
        grid=(nstage + n // tm,),
        in_specs=[tile(d), tile(NA_WIDTH), _const_spec((1, NA_WIDTH)), tile(GLA_V_WIDTH),
                  tile(GLA_V_WIDTH), _const_spec((1, GLA_DV)), tile(GLA_V_WIDTH),
                  _col_chunk_spec(layer, w_out.shape[1], d // CAST_CHUNK),
                  _const_spec((1, d)), _col_chunk_spec(layer, d, nstage),
                  _col_chunk_spec(layer, d, nstage), _row_chunk_spec(layer, d, nstage),
                  _const_spec((1, d))],
        out_specs=tile(d),
        out_shape=jax.ShapeDtypeStruct((n, d), F32),
        scratch_shapes=[pltpu.VMEM(w_out.shape[1:], BF16), pltpu.VMEM((d, d_ff), BF16),
                        pltpu.VMEM((d, d_ff), BF16), pltpu.VMEM((d_ff, d), BF16)],
        compiler_params=_params("arbitrary"),
        name="mix_ffn_final" if final_norm else "mix_ffn",
    )(x, na_o, na_gain, o_f, o_b, gla_gain, g_r, w_out, norm_g, wg, wu, wd, final_g)


def _gate_weights(w_f, b_f, w_b, b_b):
    w = jnp.zeros((CODE_PAD, 2 * GLA_QK_WIDTH), F32)
    w = w.at[:GLA_GATE_RANK, :GLA_QK_WIDTH].set(w_f)
    w = w.at[GLA_GATE_RANK:2 * GLA_GATE_RANK, GLA_QK_WIDTH:].set(w_b)
    return w, jnp.concatenate([b_f, b_b])[None, :]


def kernel(x, ffn1_norm, ffn1_wg, ffn1_wu, ffn1_wd, mix_norm, w_in, na_rpb, na_gain, w_gate_f, b_gate_f, w_gate_b, b_gate_b, gla_gain, w_out, ffn2_norm, ffn2_wg, ffn2_wu, ffn2_wd, final_norm):
    bsz, seq, d = x.shape
    depth = ffn1_norm.shape[0]
    n = bsz * seq
    xt = x.reshape(n, d)
    bias = _na_bias_tables(na_rpb)
    w_in_t = jnp.swapaxes(w_in, 1, 2)
    for l in range(depth):
        w_code = jnp.pad(w_in_t[l, MAIN_WIDTH:, :], ((0, CODE_PAD - 2 * GLA_GATE_RANK), (0, 0)))
        w_gate, b_gate = _gate_weights(w_gate_f[l], b_gate_f[l], w_gate_b[l], b_gate_b[l])
        xt, na_q, na_k, na_v, g_q, g_k, g_v, g_r, lg, min_decay = _ffn_inproj(
            xt, ffn1_norm[l][None], ffn1_wg, ffn1_wu, ffn1_wd, mix_norm[l][None], w_in_t, w_code,
            w_gate, b_gate, l)
        to3 = lambda a: a.reshape(bsz, seq, a.shape[-1])
        na_o = _na(to3(na_q), to3(na_k), to3(na_v), bias[l])
        o_f, o_b = lax.cond(jnp.min(min_decay) >= -GLA_SAFE_DECAY,
                            functools.partial(_gla, exact=False), functools.partial(_gla, exact=True),
                            to3(g_q), to3(g_k), to3(g_v), to3(lg))
        xt = _mix_ffn(xt, na_o.reshape(n, NA_WIDTH), na_gain[l][None], o_f.reshape(n, GLA_V_WIDTH),
                      o_b.reshape(n, GLA_V_WIDTH), gla_gain[l][None], g_r, w_out, ffn2_norm[l][None],
                      ffn2_wg, ffn2_wu, ffn2_wd, l, final_norm[None], final_norm=l == depth - 1)
    return xt.reshape(bsz, seq, d)
```

```python
import functools

import numpy as np
import jax
import jax.numpy as jnp
from jax import lax
from jax.experimental import pallas as pl
from jax.experimental.pallas import tpu as pltpu

F32 = jnp.float32
BF16 = jnp.bfloat16

EPS = 1e-6
NEG_INF = -1e30

GRID_W = 64
WIN_H = 8
WIN_W = 16
NA_HEADS = 8
NA_HEAD_DIM = 64
NA_WIDTH = NA_HEADS * NA_HEAD_DIM
GLA_HEADS = 4
GLA_DK = 64
GLA_DV = 128
GLA_QK_WIDTH = GLA_HEADS * GLA_DK
GLA_V_WIDTH = GLA_HEADS * GLA_DV
GLA_GATE_RANK = 16
GLA_TAU = 16.0
GLA_CHUNK = 64
MAIN_WIDTH = 3 * NA_WIDTH + 2 * GLA_QK_WIDTH + 2 * GLA_V_WIDTH

LANES = 128
SUBLANES = 8
GLA_SAFE_DECAY = 60.0
CODE_PAD = LANES
VMEM_LIMIT = 56 * 1024 * 1024

TM_TOKENS = 512
TOKEN_SUBTILE = 256
PROJ_DOT_WIDTH = 1024
FFN_DOT_WIDTH = 1024
CAST_CHUNK = 256
NA_ROWS_PER_STEP = 64
NA_LOOKAHEAD = 3
GLA_BLOCK = 1024
GLA_SUB = 256


def _rmsnorm(x, g):
    return x * lax.rsqrt(jnp.mean(x * x, axis=-1, keepdims=True) + EPS) * g


def _dot(a, b):
    return jnp.dot(a, b, preferred_element_type=F32)


def _dot_nt(a, b):
    return lax.dot_general(a, b, (((1,), (1,)), ((), ())), preferred_element_type=F32)


def _const_spec(shape):
    return pl.BlockSpec(shape, lambda *_: (0,) * len(shape), pipeline_mode=pl.Buffered(1))


def _params(*sem):
    return pltpu.CompilerParams(dimension_semantics=sem, vmem_limit_bytes=VMEM_LIMIT)


def _ffn_chunks(d_ff):
    chunks, start = [], 0
    while start < d_ff:
        width = min(FFN_DOT_WIDTH, d_ff - start)
        chunks.append((start, width))
        start += width
    return tuple(chunks)


def _swiglu_update(x, g_ref, wg_ref, wu_ref, wd_ref):
    xb = _rmsnorm(x, g_ref[...]).astype(BF16)
    y = None
    for start, width in _ffn_chunks(wg_ref.shape[1]):
        h = _dot(xb, wg_ref[:, start:start + width])
        u = _dot(xb, wu_ref[:, start:start + width])
        a = (h * jax.nn.sigmoid(h) * u).astype(BF16)
        part = _dot(a, wd_ref[start:start + width, :])
        y = part if y is None else y + part
    return x + 0.5 * y


def _stage_weight_chunks(step, stages):
    for chunk_ref, scratch_ref, axis, nchunks in stages:
        for c in range(nchunks):
            @pl.when(step == c)
            def _(chunk_ref=chunk_ref, scratch_ref=scratch_ref, axis=axis, c=c):
                span = slice(c * CAST_CHUNK, (c + 1) * CAST_CHUNK)
                if axis == 0:
                    scratch_ref[span, :] = chunk_ref[...].astype(BF16)
                elif axis == 1:
                    scratch_ref[:, span] = chunk_ref[...].astype(BF16)
                else:
                    scratch_ref[:, span] = chunk_ref[...].T.astype(BF16)


def _col_chunk_spec(layer, rows, nchunks):
    return pl.BlockSpec((None, rows, CAST_CHUNK), lambda i: (layer, 0, jnp.minimum(i, nchunks - 1)))


def _row_chunk_spec(layer, cols, nchunks):
    return pl.BlockSpec((None, CAST_CHUNK, cols), lambda i: (layer, jnp.minimum(i, nchunks - 1), 0))


def _token_tile(tm, width, nstage):
    return pl.BlockSpec((tm, width), lambda i: (jnp.maximum(i - nstage, 0), 0))


def _log_sigmoid(x):
    return jnp.minimum(x, 0.0) - jnp.log(1.0 + jnp.exp(-jnp.abs(x)))


def _chunk_prefix_sums(x):
    row = lax.broadcasted_iota(jnp.int32, x.shape, 0) & (GLA_CHUNK - 1)
    shift = 1
    while shift < GLA_CHUNK:
        x = x + jnp.where(row >= shift, pltpu.roll(x, shift, axis=0), 0.0)
        shift *= 2
    return x


def _project_rows(x, rows, g_ref, wm_ref, wc_ref, wgate, bgate_ref, outs, decay_ref):
    xb = _rmsnorm(x, g_ref[...]).astype(BF16)
    codes = _dot(xb, wc_ref[...])
    logits = _dot(codes.astype(BF16), wgate) + bgate_ref[...]
    lg = _log_sigmoid(logits) * (1.0 / GLA_TAU)
    prefix = _chunk_prefix_sums(lg)
    totals = jnp.concatenate(
        [jnp.broadcast_to(prefix[c + GLA_CHUNK - 1:c + GLA_CHUNK], (GLA_CHUNK, lg.shape[1]))
         for c in range(0, lg.shape[0], GLA_CHUNK)], axis=0)
    suffix = totals - prefix + lg
    decay_ref[rows, :] = jnp.concatenate([prefix[:, :GLA_QK_WIDTH], suffix[:, GLA_QK_WIDTH:]], axis=1)
    min_decay = jnp.min(jnp.min(totals, axis=0, keepdims=True), axis=1, keepdims=True)
    col = 0
    pending = list(outs)
    while pending:
        group, width = [], 0
        while pending and width + pending[0][1] <= PROJ_DOT_WIDTH:
            group.append(pending.pop(0))
            width += group[-1][1]
        p = _dot(xb, wm_ref[:, col:col + width])
        col += width
        off = 0
        for ref, w, scale in group:
            piece = p[:, off:off + w]
            if scale is not None:
                piece = piece * scale
            ref[rows, :] = piece.astype(ref.dtype)
            off += w
    return min_decay


def _ffn_inproj_body(x_ref, g1_ref, wgc_ref, wuc_ref, wdc_ref, g2_ref, wmc_ref, wc_ref,
                     wgate_ref, bgate_ref,
                     xo_ref, naq_ref, nak_ref, nav_ref, gq_ref, gk_ref, gv_ref, gr_ref, decay_ref,
                     md_ref, wg_ref, wu_ref, wd_ref, wm_ref, wcs_ref):
    step = pl.program_id(0)
    nffn = wg_ref.shape[1] // CAST_CHUNK
    nproj = wm_ref.shape[1] // CAST_CHUNK
    _stage_weight_chunks(step, ((wgc_ref, wg_ref, 1, nffn), (wuc_ref, wu_ref, 1, nffn),
                                (wdc_ref, wd_ref, 0, nffn), (wmc_ref, wm_ref, "1t", nproj)))

    @pl.when(step == 0)
    def _():
        wcs_ref[...] = wc_ref[...].T.astype(BF16)

    @pl.when(step >= max(nffn, nproj))
    def _():
        outs = ((naq_ref, NA_WIDTH, NA_HEAD_DIM ** -0.5), (nak_ref, NA_WIDTH, None),
                (nav_ref, NA_WIDTH, None), (gq_ref, GLA_QK_WIDTH, GLA_DK ** -0.5),
                (gk_ref, GLA_QK_WIDTH, None), (gv_ref, GLA_V_WIDTH, None),
                (gr_ref, GLA_V_WIDTH, None))
        wgate = wgate_ref[...].astype(BF16)
        subtiles = [slice(r, r + TOKEN_SUBTILE) for r in range(0, TM_TOKENS, TOKEN_SUBTILE)]
        for rows in subtiles:
            xo_ref[rows, :] = _swiglu_update(x_ref[rows, :], g1_ref, wg_ref, wu_ref, wd_ref)
        mins = [_project_rows(xo_ref[rows, :], rows, g2_ref, wm_ref, wcs_ref, wgate, bgate_ref,
                              outs, decay_ref) for rows in subtiles]
        md_ref[...] = jnp.broadcast_to(functools.reduce(jnp.minimum, mins), md_ref.shape)


def _ffn_inproj(x, ffn_g, wg, wu, wd, mix_g, w_in_t, w_code, w_gate, b_gate, layer):
    n, d = x.shape
    d_ff = wg.shape[2]
    nffn = d_ff // CAST_CHUNK
    nproj = MAIN_WIDTH // CAST_CHUNK
    nstage = max(nffn, nproj)

    def tile(width):
        return _token_tile(TM_TOKENS, width, nstage)

    widths = (d, NA_WIDTH, NA_WIDTH, NA_WIDTH, GLA_QK_WIDTH, GLA_QK_WIDTH, GLA_V_WIDTH,
              GLA_V_WIDTH, 2 * GLA_QK_WIDTH)
    dtypes = (F32, BF16, BF16, BF16, F32, F32, BF16, F32, F32)
    return pl.pallas_call(
        _ffn_inproj_body,
        grid=(nstage + n // TM_TOKENS,),
        in_specs=[tile(d), _const_spec((1, d)), _col_chunk_spec(layer, d, nffn),
                  _col_chunk_spec(layer, d, nffn), _row_chunk_spec(layer, d, nffn),
                  _const_spec((1, d)), _row_chunk_spec(layer, d, nproj),
                  _const_spec(w_code.shape), _const_spec(w_gate.shape), _const_spec(b_gate.shape)],
        out_specs=[tile(w) for w in widths] + [_token_tile(SUBLANES, LANES, nstage)],
        out_shape=[jax.ShapeDtypeStruct((n, w), dt) for w, dt in zip(widths, dtypes)]
        + [jax.ShapeDtypeStruct((n // TM_TOKENS * SUBLANES, LANES), F32)],
        scratch_shapes=[pltpu.VMEM((d, d_ff), BF16), pltpu.VMEM((d, d_ff), BF16),
                        pltpu.VMEM((d_ff, d), BF16), pltpu.VMEM((d, MAIN_WIDTH), BF16),
                        pltpu.VMEM((d, CODE_PAD), BF16)],
        compiler_params=_params("arbitrary"),
        name="ffn_inproj",
    )(x, ffn_g, wg, wu, wd, mix_g, w_in_t, w_code, w_gate, b_gate)


_NA_ROW_PAIRS = 2 * WIN_H - 2


def _na_bias_tables(rpb):
    ncol = 2 * WIN_W - 1
    qc = np.arange(GRID_W)
    kc = np.arange(GRID_W)
    col_start = np.clip(qc - WIN_W // 2, 0, GRID_W - WIN_W)
    valid = (kc[None, :] >= col_start[:, None]) & (kc[None, :] < col_start[:, None] + WIN_W)
    dc = np.clip(kc[None, :] - qc[:, None] + (WIN_W - 1), 0, ncol - 1)
    onehot = (np.arange(ncol)[:, None, None] == dc[None]).astype(np.float32)
    selector = np.einsum("ab,cqk->acqbk", np.eye(2, dtype=np.float32), onehot)
    selector = selector.reshape(2 * ncol, GRID_W * 2 * GRID_W)
    pairs = jnp.concatenate([rpb[:, :, :-1], rpb[:, :, 1:]], axis=-1).astype(F32)
    lead = pairs.shape[:-1]
    t = jnp.dot(pairs.reshape(-1, 2 * ncol), selector, precision=lax.Precision.HIGHEST)
    t = t.reshape(*lead, GRID_W, 2 * GRID_W)
    return jnp.where(np.concatenate([valid, valid], axis=-1), t, NEG_INF)


def _na_body(q_ref, k_ref, v_ref, bias_ref, o_ref, s_ref, *, nrows):
    rb = pl.program_id(2)
    half = WIN_H // 2
    lane = lax.broadcasted_iota(jnp.int32, (GRID_W, LANES), 1)
    first_head = lane < NA_HEAD_DIM
    nkeys = WIN_H * GRID_W

    def key_offset(i):
        r = rb * NA_ROWS_PER_STEP + i
        row_start = jnp.clip(r - half, 0, nrows - WIN_H)
        return r, row_start, pl.multiple_of(row_start * GRID_W, GRID_W)

    def scores(i):
        _, _, koff = key_offset(i)
        qr = q_ref[0, i * GRID_W:(i + 1) * GRID_W, :]
        zero = jnp.zeros_like(qr)
        qs = jnp.concatenate([jnp.where(first_head, qr, zero), jnp.where(first_head, zero, qr)], axis=0)
        return _dot_nt(qs, k_ref[0, pl.ds(koff, nkeys), :])

    def finish(i):
        r, row_start, koff = key_offset(i)
        dr0 = row_start - r + (WIN_H - 1)
        bias = jnp.concatenate(
            [jnp.concatenate([bias_ref[hl, dr0 + 2 * wp] for wp in range(WIN_H // 2)], axis=1)
             for hl in range(2)], axis=0)
        s = s_ref[i % (NA_LOOKAHEAD + 1)] + bias
        e = jnp.exp(s - jnp.max(s, axis=-1, keepdims=True))
        o = _dot(e.astype(BF16), v_ref[0, pl.ds(koff, nkeys), :])
        o = o / jnp.sum(e, axis=-1, keepdims=True)
        o_ref[0, i * GRID_W:(i + 1) * GRID_W, :] = jnp.where(first_head, o[:GRID_W], o[GRID_W:])

    for i in range(NA_LOOKAHEAD):
        s_ref[i] = scores(i)
    for i in range(NA_ROWS_PER_STEP):
        ahead = i + NA_LOOKAHEAD
        if ahead < NA_ROWS_PER_STEP:
            s_ref[ahead % (NA_LOOKAHEAD + 1)] = scores(ahead)
        finish(i)


def _na(q, k, v, bias):
    b, s, _ = q.shape
    nrows = s // GRID_W
    qrows = NA_ROWS_PER_STEP * GRID_W
    return pl.pallas_call(
        functools.partial(_na_body, nrows=nrows),
        grid=(NA_HEADS // 2, b, nrows // NA_ROWS_PER_STEP),
        in_specs=[
            pl.BlockSpec((1, qrows, LANES), lambda hp, bi, rb: (bi, rb, hp)),
            pl.BlockSpec((1, s, LANES), lambda hp, bi, rb: (bi, 0, hp)),
            pl.BlockSpec((1, s, LANES), lambda hp, bi, rb: (bi, 0, hp)),
            pl.BlockSpec((2, _NA_ROW_PAIRS, GRID_W, LANES), lambda hp, bi, rb: (hp, 0, 0, 0)),
        ],
        out_specs=pl.BlockSpec((1, qrows, LANES), lambda hp, bi, rb: (bi, rb, hp)),
        out_shape=jax.ShapeDtypeStruct((b, s, NA_WIDTH), F32),
        scratch_shapes=[pltpu.VMEM((NA_LOOKAHEAD + 1, 2 * GRID_W, WIN_H * GRID_W), F32)],
        compiler_params=_params("parallel", "parallel", "arbitrary"),
        name="na",
    )(q, k, v, bias)


class _GlaSubBlock:
    def __init__(self, q_ref, k_ref, v_ref, b_ref, o_ref, s_ref, a_ref, u_ref, vrow_ref, *,
                 r0, slot, a_mask, forward, exact):
        self.q_ref, self.k_ref, self.v_ref, self.b_ref = q_ref, k_ref, v_ref, b_ref
        self.o_ref, self.s_ref, self.a_ref, self.u_ref = o_ref, s_ref, a_ref, u_ref
        self.vrow_ref = vrow_ref
        self.r0, self.slot = r0, slot
        self.rows = slice(r0, r0 + GLA_SUB)
        self.a_mask, self.forward, self.exact = a_mask, forward, exact
        nchunk = GLA_SUB // GLA_CHUNK
        self.last = GLA_CHUNK - 1 if forward else 0
        self.order = tuple(range(nchunk)) if forward else tuple(reversed(range(nchunk)))

    def scores(self, head_masks, head_expand):
        b = self.b_ref[0, self.rows, :]
        q = self.q_ref[0, self.rows, :]
        self.qt = (q * jnp.exp(b)).astype(BF16)
        if not self.exact:
            kt = (self.k_ref[0, self.rows, :] * jnp.exp(-b)).astype(BF16)
        self.qs = {}
        for c in range(GLA_SUB // GLA_CHUNK):
            rows = slice(c * GLA_CHUNK, (c + 1) * GLA_CHUNK)
            qc = self.qt[rows]
            qs = jnp.concatenate([jnp.where(hm, qc, jnp.zeros_like(qc)) for hm in head_masks], axis=0)
            self.qs[c] = qs
            if self.exact:
                self.a_ref[self.slot, rows, :] = self._exact_intra(c, q[rows], b[rows], head_expand)
            else:
                a = _dot_nt(qs, kt[rows])
                self.a_ref[self.slot, c] = jnp.where(self.a_mask, a, 0.0).astype(BF16)

    def _exact_intra(self, c, qc, bc, head_expand):
        base = self.r0 + c * GLA_CHUNK
        self.vrow_ref[...] = self.v_ref[0, base:base + GLA_CHUNK, :].astype(F32)
        t = lax.broadcasted_iota(jnp.int32, (GLA_CHUNK, 1), 0)

        def body(s, acc):
            krow = self.k_ref[0, pl.ds(base + s, 1), :]
            brow = self.b_ref[0, pl.ds(base + s, 1), :]
            valid = (t >= s) if self.forward else (t < s)
            decay = jnp.exp(jnp.where(valid, bc - brow, 0.0))
            x = jnp.where(valid, qc * krow * decay, 0.0).astype(BF16)
            w = _dot(x, head_expand)
            return acc + w * self.vrow_ref[pl.ds(s, 1), :]

        return lax.fori_loop(0, GLA_CHUNK, body, jnp.zeros((GLA_CHUNK, GLA_V_WIDTH), F32))

    def increments(self, chunk_masks):
        nchunk = GLA_SUB // GLA_CHUNK
        b = self.b_ref[0, self.rows, :]
        b_last = jnp.concatenate(
            [jnp.broadcast_to(b[c * GLA_CHUNK + self.last:c * GLA_CHUNK + self.last + 1, :],
                              (GLA_CHUNK, GLA_QK_WIDTH)) for c in range(nchunk)], axis=0)
        ku_t = (self.k_ref[0, self.rows, :] * jnp.exp(b_last - b)).T.astype(BF16)
        self.decay_t = jnp.exp(b_last.T)
        for c in range(nchunk):
            pair = slice((c // 2) * 2 * GLA_CHUNK, (c // 2 + 1) * 2 * GLA_CHUNK)
            vrows = slice(self.r0 + pair.start, self.r0 + pair.stop)
            ku_pair = ku_t[:, pair]
            ku_c = jnp.where(chunk_masks[c % 2], ku_pair, jnp.zeros_like(ku_pair))
            for h in range(GLA_HEADS):
                hrows = slice(h * GLA_DK, (h + 1) * GLA_DK)
                vcols = slice(h * GLA_DV, (h + 1) * GLA_DV)
                self.u_ref[self.slot, c, hrows, :] = _dot(ku_c[hrows], self.v_ref[0, vrows, vcols])

    def advance_state(self):
        s = self.s_ref[...]
        self.states = {}
        for c in self.order:
            self.states[c] = s.astype(BF16)
            col = c * GLA_CHUNK
            s = self.decay_t[:, col:col + 1] * s + self.u_ref[self.slot, c]
        self.s_ref[...] = s

    def output_chunk(self, c):
        sub_rows = slice(c * GLA_CHUNK, (c + 1) * GLA_CHUNK)
        rows = slice(self.r0 + sub_rows.start, self.r0 + sub_rows.stop)
        inter = _dot(self.qs[c], self.states[c])
        for h in range(GLA_HEADS):
            arows = slice(h * GLA_CHUNK, (h + 1) * GLA_CHUNK)
            vcols = slice(h * GLA_DV, (h + 1) * GLA_DV)
            if self.exact:
                intra = self.a_ref[self.slot, sub_rows, vcols]
            else:
                intra = _dot(self.a_ref[self.slot, c, arows, :], self.v_ref[0, rows, vcols])
            self.o_ref[0, rows, vcols] = inter[arows] + intra


def _gla_body(exact, qf_ref, kf_ref, vf_ref, bf_ref, qb_ref, kb_ref, vb_ref, bb_ref,
              of_ref, ob_ref, sf_ref, sb_ref, af_ref, ab_ref, uf_ref, ub_ref, vrow_ref):
    @pl.when(pl.program_id(1) == 0)
    def _():
        sf_ref[...] = jnp.zeros_like(sf_ref)
        sb_ref[...] = jnp.zeros_like(sb_ref)

    shift = GLA_CHUNK.bit_length() - 1
    nsub = GLA_BLOCK // GLA_SUB
    lane = lax.broadcasted_iota(jnp.int32, (GLA_CHUNK, GLA_QK_WIDTH), 1)
    head_masks = [(lane >> shift) == h for h in range(GLA_HEADS)]
    pair_lane = lax.broadcasted_iota(jnp.int32, (GLA_QK_WIDTH, 2 * GLA_CHUNK), 1)
    chunk_masks = [(pair_lane >> shift) == j for j in range(2)]
    t = lax.broadcasted_iota(jnp.int32, (GLA_HEADS * GLA_CHUNK, GLA_CHUNK), 0) & (GLA_CHUNK - 1)
    s = lax.broadcasted_iota(jnp.int32, (GLA_HEADS * GLA_CHUNK, GLA_CHUNK), 1)
    head_expand = None
    if exact:
        erow = lax.broadcasted_iota(jnp.int32, (GLA_QK_WIDTH, GLA_V_WIDTH), 0) // GLA_DK
        ecol = lax.broadcasted_iota(jnp.int32, (GLA_QK_WIDTH, GLA_V_WIDTH), 1) // GLA_DV
        head_expand = jnp.where(erow == ecol, 1.0, 0.0).astype(BF16)
    fwd = [_GlaSubBlock(qf_ref, kf_ref, vf_ref, bf_ref, of_ref, sf_ref, af_ref, uf_ref, vrow_ref,
                        r0=i * GLA_SUB, slot=i, a_mask=t >= s, forward=True, exact=exact)
           for i in range(nsub)]
    bwd = [_GlaSubBlock(qb_ref, kb_ref, vb_ref, bb_ref, ob_ref, sb_ref, ab_ref, ub_ref, vrow_ref,
                        r0=i * GLA_SUB, slot=i, a_mask=t < s, forward=False, exact=exact)
           for i in reversed(range(nsub))]
    both = [blk for pair in zip(fwd, bwd) for blk in pair]
    for blk in both:
        blk.scores(head_masks, head_expand)
    for blk in both:
        blk.increments(chunk_masks)
    for blk in both:
        blk.advance_state()
    for f, b in zip(fwd, bwd):
        for cf, cb in zip(f.order, b.order):
            f.output_chunk(cf)
            b.output_chunk(cb)


def _gla(q, k, v, decay, *, exact):
    b, s, _ = q.shape
    nb = s // GLA_BLOCK

    def fwd(width, j=0):
        return pl.BlockSpec((1, GLA_BLOCK, width), lambda bi, n: (bi, n, j))

    def bwd(width, j=0):
        return pl.BlockSpec((1, GLA_BLOCK, width), lambda bi, n: (bi, nb - 1 - n, j))

    nsub = GLA_BLOCK // GLA_SUB
    nchunk = GLA_SUB // GLA_CHUNK
    state = pltpu.VMEM((GLA_QK_WIDTH, GLA_DV), F32)
    if exact:
        intra = pltpu.VMEM((nsub, GLA_SUB, GLA_V_WIDTH), F32)
    else:
        intra = pltpu.VMEM((nsub, nchunk, GLA_HEADS * GLA_CHUNK, GLA_CHUNK), BF16)
    increments = pltpu.VMEM((nsub, nchunk, GLA_QK_WIDTH, GLA_DV), F32)
    vrows = pltpu.VMEM((GLA_CHUNK, GLA_V_WIDTH), F32)
    return pl.pallas_call(
        functools.partial(_gla_body, exact),
        grid=(b, nb),
        in_specs=[fwd(GLA_QK_WIDTH), fwd(GLA_QK_WIDTH), fwd(GLA_V_WIDTH), fwd(GLA_QK_WIDTH, 0),
                  bwd(GLA_QK_WIDTH), bwd(GLA_QK_WIDTH), bwd(GLA_V_WIDTH), bwd(GLA_QK_WIDTH, 1)],
        out_specs=[fwd(GLA_V_WIDTH), bwd(GLA_V_WIDTH)],
        out_shape=[jax.ShapeDtypeStruct((b, s, GLA_V_WIDTH), F32)] * 2,
        scratch_shapes=[state, state, intra, intra, increments, increments, vrows],
        compiler_params=_params("parallel", "arbitrary"),
        name="gla_exact" if exact else "gla",
    )(q, k, v, decay, q, k, v, decay)


def _mix_update(x, rows, na_ref, nag_ref, of_ref, ob_ref, gg_ref, gr_ref, wo_ref):
    na = _rmsnorm(na_ref[rows, :], nag_ref[...]).astype(BF16)
    y = _dot(na, wo_ref[:NA_WIDTH, :])
    o = of_ref[rows, :] + ob_ref[rows, :]
    r = gr_ref[rows, :]
    gate = r * jax.nn.sigmoid(r)
    heads = []
    for h in range(GLA_HEADS):
        cols = slice(h * GLA_DV, (h + 1) * GLA_DV)
        heads.append((_rmsnorm(o[:, cols], gg_ref[...]) * gate[:, cols]).astype(BF16))
    return x + y + _dot(jnp.concatenate(heads, axis=-1), wo_ref[NA_WIDTH:, :])


def _mix_ffn_body(final_norm, x_ref, na_ref, nag_ref, of_ref, ob_ref, gg_ref, gr_ref, woc_ref,
                  g_ref, wgc_ref, wuc_ref, wdc_ref, fg_ref, o_ref, wo_ref, wg_ref, wu_ref, wd_ref):
    step = pl.program_id(0)
    nstage = wg_ref.shape[1] // CAST_CHUNK
    _stage_weight_chunks(step, ((woc_ref, wo_ref, 1, wo_ref.shape[1] // CAST_CHUNK),
                                (wgc_ref, wg_ref, 1, nstage), (wuc_ref, wu_ref, 1, nstage),
                                (wdc_ref, wd_ref, 0, nstage)))

    @pl.when(step >= nstage)
    def _():
        subtiles = [slice(r, r + TOKEN_SUBTILE) for r in range(0, TM_TOKENS, TOKEN_SUBTILE)]
        for rows in subtiles:
            o_ref[rows, :] = _mix_update(x_ref[rows, :], rows, na_ref, nag_ref, of_ref, ob_ref,
                                         gg_ref, gr_ref, wo_ref)
        for rows in subtiles:
            y = _swiglu_update(o_ref[rows, :], g_ref, wg_ref, wu_ref, wd_ref)
            o_ref[rows, :] = _rmsnorm(y, fg_ref[...]) if final_norm else y


def _mix_ffn(x, na_o, na_gain, o_f, o_b, gla_gain, g_r, w_out, norm_g, wg, wu, wd, layer,
             final_g, final_norm):
    n, d = x.shape
    d_ff = wg.shape[2]
    tm = TM_TOKENS
    nstage = d_ff // CAST_CHUNK

    def tile(width):
        return _token_tile(tm, width, nstage)

    return pl.pallas_call(
        functools.partial(_mix_ffn_body, final_norm),
        grid=(nstage + n // tm,),
        in_specs=[tile(d), tile(NA_WIDTH), _const_spec((1, NA_WIDTH)), tile(GLA_V_WIDTH),
                  tile(GLA_V_WIDTH), _const_spec((1, GLA_DV)), tile(GLA_V_WIDTH),
                  _col_chunk_spec(layer, w_out.shape[1], d // CAST_CHUNK),
                  _const_spec((1, d)), _col_chunk_spec(layer, d, nstage),
                  _col_chunk_spec(layer, d, nstage), _row_chunk_spec(layer, d, nstage),
                  _const_spec((1, d))],
        out_specs=tile(d),
        out_shape=jax.ShapeDtypeStruct((n, d), F32),
        scratch_shapes=[pltpu.VMEM(w_out.shape[1:], BF16), pltpu.VMEM((d, d_ff), BF16),
                        pltpu.VMEM((d, d_ff), BF16), pltpu.VMEM((d_ff, d), BF16)],
        compiler_params=_params("arbitrary"),
        name="mix_ffn_final" if final_norm else "mix_ffn",
    )(x, na_o, na_gain, o_f, o_b, gla_gain, g_r, w_out, norm_g, wg, wu, wd, final_g)


def _gate_weights(w_f, b_f, w_b, b_b):
    w = jnp.zeros((CODE_PAD, 2 * GLA_QK_WIDTH), F32)
    w = w.at[:GLA_GATE_RANK, :GLA_QK_WIDTH].set(w_f)
    w = w.at[GLA_GATE_RANK:2 * GLA_GATE_RANK, GLA_QK_WIDTH:].set(w_b)
    return w, jnp.concatenate([b_f, b_b])[None, :]


def kernel(x, ffn1_norm, ffn1_wg, ffn1_wu, ffn1_wd, mix_norm, w_in, na_rpb, na_gain, w_gate_f, b_gate_f, w_gate_b, b_gate_b, gla_gain, w_out, ffn2_norm, ffn2_wg, ffn2_wu, ffn2_wd, final_norm):
    bsz, seq, d = x.shape
    depth = ffn1_norm.shape[0]
    n = bsz * seq
    xt = x.reshape(n, d)
    bias = _na_bias_tables(na_rpb)
    w_in_t = jnp.swapaxes(w_in, 1, 2)
    for l in range(depth):
        w_code = jnp.pad(w_in_t[l, MAIN_WIDTH:, :], ((0, CODE_PAD - 2 * GLA_GATE_RANK), (0, 0)))
        w_gate, b_gate = _gate_weights(w_gate_f[l], b_gate_f[l], w_gate_b[l], b_gate_b[l])
        xt, na_q, na_k, na_v, g_q, g_k, g_v, g_r, decay, min_decay = _ffn_inproj(
            xt, ffn1_norm[l][None], ffn1_wg, ffn1_wu, ffn1_wd, mix_norm[l][None], w_in_t, w_code,
            w_gate, b_gate, l)
        to3 = lambda a: a.reshape(bsz, seq, a.shape[-1])
        na_o = _na(to3(na_q), to3(na_k), to3(na_v), bias[l])
        o_f, o_b = lax.cond(jnp.min(min_decay) >= -GLA_SAFE_DECAY,
                            functools.partial(_gla, exact=False), functools.partial(_gla, exact=True),
                            to3(g_q), to3(g_k), to3(g_v), to3(decay))
        xt = _mix_ffn(xt, na_o.reshape(n, NA_WIDTH), na_gain[l][None], o_f.reshape(n, GLA_V_WIDTH),
                      o_b.reshape(n, GLA_V_WIDTH), gla_gain[l][None], g_r, w_out, ffn2_norm[l][None],
                      ffn2_wg, ffn2_wu, ffn2_wd, l, final_norm[None], final_norm=l == depth - 1)
    return xt.reshape(bsz, seq, d)
```

```python
import functools

import numpy as np
import jax
import jax.numpy as jnp
from jax import lax
from jax.experimental import pallas as pl
from jax.experimental.pallas import tpu as pltpu

F32 = jnp.float32
BF16 = jnp.bfloat16

EPS = 1e-6
NEG_INF = -1e30

GRID_W = 64
WIN_H = 8
WIN_W = 16
NA_HEADS = 8
NA_HEAD_DIM = 64
NA_WIDTH = NA_HEADS * NA_HEAD_DIM
GLA_HEADS = 4
GLA_DK = 64
GLA_DV = 128
GLA_QK_WIDTH = GLA_HEADS * GLA_DK
GLA_V_WIDTH = GLA_HEADS * GLA_DV
GLA_GATE_RANK = 16
GLA_TAU = 16.0
GLA_CHUNK = 64
MAIN_WIDTH = 3 * NA_WIDTH + 2 * GLA_QK_WIDTH + 2 * GLA_V_WIDTH
PB_NA_Q, PB_NA_K, PB_NA_V, PB_GLA_V = 0, NA_WIDTH, 2 * NA_WIDTH, 3 * NA_WIDTH
PB_WIDTH = 3 * NA_WIDTH + GLA_V_WIDTH
PF_Q, PF_K, PF_R, PF_GATES = 0, GLA_QK_WIDTH, 2 * GLA_QK_WIDTH, 2 * GLA_QK_WIDTH + GLA_V_WIDTH
PF_WIDTH = 4 * GLA_QK_WIDTH + GLA_V_WIDTH

LANES = 128
SUBLANES = 8
GLA_SAFE_DECAY = 60.0
CODE_PAD = LANES
VMEM_LIMIT = 56 * 1024 * 1024

TM_TOKENS = 512
TOKEN_SUBTILE = 256
PROJ_DOT_WIDTH = 1024
FFN_DOT_WIDTH = 1024
CAST_CHUNK = 256
NA_ROWS_PER_STEP = 64
NA_LOOKAHEAD = 3
GLA_BLOCK = 1024
GLA_SUB = 256


def _rmsnorm(x, g):
    return x * lax.rsqrt(jnp.mean(x * x, axis=-1, keepdims=True) + EPS) * g


def _dot(a, b):
    return jnp.dot(a, b, preferred_element_type=F32)


def _dot_nt(a, b):
    return lax.dot_general(a, b, (((1,), (1,)), ((), ())), preferred_element_type=F32)


def _const_spec(shape):
    return pl.BlockSpec(shape, lambda *_: (0,) * len(shape), pipeline_mode=pl.Buffered(1))


def _params(*sem):
    return pltpu.CompilerParams(dimension_semantics=sem, vmem_limit_bytes=VMEM_LIMIT)


def _ffn_chunks(d_ff):
    chunks, start = [], 0
    while start < d_ff:
        width = min(FFN_DOT_WIDTH, d_ff - start)
        chunks.append((start, width))
        start += width
    return tuple(chunks)


def _swiglu_update(x, g_ref, wg_ref, wu_ref, wd_ref):
    xb = _rmsnorm(x, g_ref[...]).astype(BF16)
    y = None
    for start, width in _ffn_chunks(wg_ref.shape[1]):
        h = _dot(xb, wg_ref[:, start:start + width])
        u = _dot(xb, wu_ref[:, start:start + width])
        a = (h * jax.nn.sigmoid(h) * u).astype(BF16)
        part = _dot(a, wd_ref[start:start + width, :])
        y = part if y is None else y + part
    return x + 0.5 * y


def _stage_weight_chunks(step, stages):
    for chunk_ref, scratch_ref, axis, nchunks in stages:
        for c in range(nchunks):
            @pl.when(step == c)
            def _(chunk_ref=chunk_ref, scratch_ref=scratch_ref, axis=axis, c=c):
                span = slice(c * CAST_CHUNK, (c + 1) * CAST_CHUNK)
                if axis == 0:
                    scratch_ref[span, :] = chunk_ref[...].astype(BF16)
                elif axis == 1:
                    scratch_ref[:, span] = chunk_ref[...].astype(BF16)
                else:
                    scratch_ref[:, span] = chunk_ref[...].T.astype(BF16)


def _col_chunk_spec(layer, rows, nchunks):
    return pl.BlockSpec((None, rows, CAST_CHUNK), lambda i: (layer, 0, jnp.minimum(i, nchunks - 1)))


def _row_chunk_spec(layer, cols, nchunks):
    return pl.BlockSpec((None, CAST_CHUNK, cols), lambda i: (layer, jnp.minimum(i, nchunks - 1), 0))


def _token_tile(tm, width, nstage, col=0):
    return pl.BlockSpec((tm, width), lambda i: (jnp.maximum(i - nstage, 0), col))


def _log_sigmoid(x):
    return jnp.minimum(x, 0.0) - jnp.log(1.0 + jnp.exp(-jnp.abs(x)))


def _project_rows(x, rows, g_ref, wm_ref, wc_ref, wgate, bgate_ref, outs, lg_out):
    xb = _rmsnorm(x, g_ref[...]).astype(BF16)
    codes = _dot(xb, wc_ref[...])
    logits = _dot(codes.astype(BF16), wgate) + bgate_ref[...]
    lg = _log_sigmoid(logits) * (1.0 / GLA_TAU)
    lg_out[0][rows, lg_out[1]:lg_out[1] + lg.shape[1]] = lg
    chunk_sums = [jnp.sum(lg[c:c + GLA_CHUNK], axis=0, keepdims=True)
                  for c in range(0, lg.shape[0], GLA_CHUNK)]
    min_decay = jnp.min(functools.reduce(jnp.minimum, chunk_sums), axis=1, keepdims=True)
    col = 0
    pending = list(outs)
    while pending:
        group, width = [], 0
        while pending and width + pending[0][2] <= PROJ_DOT_WIDTH:
            group.append(pending.pop(0))
            width += group[-1][2]
        p = _dot(xb, wm_ref[:, col:col + width])
        col += width
        off = 0
        for ref, first, w, scale in group:
            piece = p[:, off:off + w]
            if scale is not None:
                piece = piece * scale
            ref[rows, first:first + w] = piece.astype(ref.dtype)
            off += w
    return min_decay


def _ffn_inproj_body(x_ref, g1_ref, wgc_ref, wuc_ref, wdc_ref, g2_ref, wmc_ref, wc_ref,
                     wgate_ref, bgate_ref,
                     xo_ref, pb_ref, pf_ref, md_ref, wg_ref, wu_ref, wd_ref, wm_ref, wcs_ref):
    step = pl.program_id(0)
    nffn = wg_ref.shape[1] // CAST_CHUNK
    nproj = wm_ref.shape[1] // CAST_CHUNK
    _stage_weight_chunks(step, ((wgc_ref, wg_ref, 1, nffn), (wuc_ref, wu_ref, 1, nffn),
                                (wdc_ref, wd_ref, 0, nffn), (wmc_ref, wm_ref, "1t", nproj)))

    @pl.when(step == 0)
    def _():
        wcs_ref[...] = wc_ref[...].T.astype(BF16)

    @pl.when(step >= max(nffn, nproj))
    def _():
        outs = ((pb_ref, PB_NA_Q, NA_WIDTH, NA_HEAD_DIM ** -0.5), (pb_ref, PB_NA_K, NA_WIDTH, None),
                (pb_ref, PB_NA_V, NA_WIDTH, None), (pf_ref, PF_Q, GLA_QK_WIDTH, GLA_DK ** -0.5),
                (pf_ref, PF_K, GLA_QK_WIDTH, None), (pb_ref, PB_GLA_V, GLA_V_WIDTH, None),
                (pf_ref, PF_R, GLA_V_WIDTH, None))
        wgate = wgate_ref[...].astype(BF16)
        subtiles = [slice(r, r + TOKEN_SUBTILE) for r in range(0, TM_TOKENS, TOKEN_SUBTILE)]
        for rows in subtiles:
            xo_ref[rows, :] = _swiglu_update(x_ref[rows, :], g1_ref, wg_ref, wu_ref, wd_ref)
        mins = [_project_rows(xo_ref[rows, :], rows, g2_ref, wm_ref, wcs_ref, wgate, bgate_ref,
                              outs, (pf_ref, PF_GATES)) for rows in subtiles]
        md_ref[...] = jnp.broadcast_to(functools.reduce(jnp.minimum, mins), md_ref.shape)


def _ffn_inproj(x, ffn_g, wg, wu, wd, mix_g, w_in_t, w_code, w_gate, b_gate, layer):
    n, d = x.shape
    d_ff = wg.shape[2]
    nffn = d_ff // CAST_CHUNK
    nproj = MAIN_WIDTH // CAST_CHUNK
    nstage = max(nffn, nproj)

    def tile(width):
        return _token_tile(TM_TOKENS, width, nstage)

    widths = (d, PB_WIDTH, PF_WIDTH)
    dtypes = (F32, BF16, F32)
    return pl.pallas_call(
        _ffn_inproj_body,
        grid=(nstage + n // TM_TOKENS,),
        in_specs=[tile(d), _const_spec((1, d)), _col_chunk_spec(layer, d, nffn),
                  _col_chunk_spec(layer, d, nffn), _row_chunk_spec(layer, d, nffn),
                  _const_spec((1, d)), _row_chunk_spec(layer, d, nproj),
                  _const_spec(w_code.shape), _const_spec(w_gate.shape), _const_spec(b_gate.shape)],
        out_specs=[tile(w) for w in widths] + [_token_tile(SUBLANES, LANES, nstage)],
        out_shape=[jax.ShapeDtypeStruct((n, w), dt) for w, dt in zip(widths, dtypes)]
        + [jax.ShapeDtypeStruct((n // TM_TOKENS * SUBLANES, LANES), F32)],
        scratch_shapes=[pltpu.VMEM((d, d_ff), BF16), pltpu.VMEM((d, d_ff), BF16),
                        pltpu.VMEM((d_ff, d), BF16), pltpu.VMEM((d, MAIN_WIDTH), BF16),
                        pltpu.VMEM((d, CODE_PAD), BF16)],
        compiler_params=_params("arbitrary"),
        name="ffn_inproj",
    )(x, ffn_g, wg, wu, wd, mix_g, w_in_t, w_code, w_gate, b_gate)


_NA_ROW_PAIRS = 2 * WIN_H - 2


def _na_bias_tables(rpb):
    ncol = 2 * WIN_W - 1
    qc = np.arange(GRID_W)
    kc = np.arange(GRID_W)
    col_start = np.clip(qc - WIN_W // 2, 0, GRID_W - WIN_W)
    valid = (kc[None, :] >= col_start[:, None]) & (kc[None, :] < col_start[:, None] + WIN_W)
    dc = np.clip(kc[None, :] - qc[:, None] + (WIN_W - 1), 0, ncol - 1)
    onehot = (np.arange(ncol)[:, None, None] == dc[None]).astype(np.float32)
    selector = np.einsum("ab,cqk->acqbk", np.eye(2, dtype=np.float32), onehot)
    selector = selector.reshape(2 * ncol, GRID_W * 2 * GRID_W)
    pairs = jnp.concatenate([rpb[:, :, :-1], rpb[:, :, 1:]], axis=-1).astype(F32)
    lead = pairs.shape[:-1]
    t = jnp.dot(pairs.reshape(-1, 2 * ncol), selector, precision=lax.Precision.HIGHEST)
    t = t.reshape(*lead, GRID_W, 2 * GRID_W)
    return jnp.where(np.concatenate([valid, valid], axis=-1), t, NEG_INF)


def _na_body(q_ref, k_ref, v_ref, bias_ref, o_ref, s_ref, *, nrows):
    rb = pl.program_id(2)
    half = WIN_H // 2
    lane = lax.broadcasted_iota(jnp.int32, (GRID_W, LANES), 1)
    first_head = lane < NA_HEAD_DIM
    nkeys = WIN_H * GRID_W

    def key_offset(i):
        r = rb * NA_ROWS_PER_STEP + i
        row_start = jnp.clip(r - half, 0, nrows - WIN_H)
        return r, row_start, pl.multiple_of(row_start * GRID_W, GRID_W)

    def scores(i):
        _, _, koff = key_offset(i)
        qr = q_ref[0, i * GRID_W:(i + 1) * GRID_W, :]
        zero = jnp.zeros_like(qr)
        qs = jnp.concatenate([jnp.where(first_head, qr, zero), jnp.where(first_head, zero, qr)], axis=0)
        return _dot_nt(qs, k_ref[0, pl.ds(koff, nkeys), :])

    def finish(i):
        r, row_start, koff = key_offset(i)
        dr0 = row_start - r + (WIN_H - 1)
        bias = jnp.concatenate(
            [jnp.concatenate([bias_ref[hl, dr0 + 2 * wp] for wp in range(WIN_H // 2)], axis=1)
             for hl in range(2)], axis=0)
        s = s_ref[i % (NA_LOOKAHEAD + 1)] + bias
        e = jnp.exp(s - jnp.max(s, axis=-1, keepdims=True))
        o = _dot(e.astype(BF16), v_ref[0, pl.ds(koff, nkeys), :])
        o = o / jnp.sum(e, axis=-1, keepdims=True)
        o_ref[0, i * GRID_W:(i + 1) * GRID_W, :] = jnp.where(first_head, o[:GRID_W], o[GRID_W:])

    for i in range(NA_LOOKAHEAD):
        s_ref[i] = scores(i)
    for i in range(NA_ROWS_PER_STEP):
        ahead = i + NA_LOOKAHEAD
        if ahead < NA_ROWS_PER_STEP:
            s_ref[ahead % (NA_LOOKAHEAD + 1)] = scores(ahead)
        finish(i)


def _na(packed, bias):
    b, s, _ = packed.shape
    nrows = s // GRID_W
    qrows = NA_ROWS_PER_STEP * GRID_W
    q, k, v = packed, packed, packed
    qb, kb, vb = PB_NA_Q // LANES, PB_NA_K // LANES, PB_NA_V // LANES
    return pl.pallas_call(
        functools.partial(_na_body, nrows=nrows),
        grid=(NA_HEADS // 2, b, nrows // NA_ROWS_PER_STEP),
        in_specs=[
            pl.BlockSpec((1, qrows, LANES), lambda hp, bi, rb: (bi, rb, qb + hp)),
            pl.BlockSpec((1, s, LANES), lambda hp, bi, rb: (bi, 0, kb + hp)),
            pl.BlockSpec((1, s, LANES), lambda hp, bi, rb: (bi, 0, vb + hp)),
            pl.BlockSpec((2, _NA_ROW_PAIRS, GRID_W, LANES), lambda hp, bi, rb: (hp, 0, 0, 0)),
        ],
        out_specs=pl.BlockSpec((1, qrows, LANES), lambda hp, bi, rb: (bi, rb, hp)),
        out_shape=jax.ShapeDtypeStruct((b, s, NA_WIDTH), F32),
        scratch_shapes=[pltpu.VMEM((NA_LOOKAHEAD + 1, 2 * GRID_W, WIN_H * GRID_W), F32)],
        compiler_params=_params("parallel", "parallel", "arbitrary"),
        name="na",
    )(q, k, v, bias)


def _split3(x):
    x1 = x.astype(BF16)
    r1 = x - x1.astype(F32)
    x2 = r1.astype(BF16)
    x3 = (r1 - x2.astype(F32)).astype(BF16)
    return x1, x2, x3


class _GlaSubBlock:
    def __init__(self, q_ref, k_ref, v_ref, g_ref, o_ref, s_ref, b_ref, a_ref, u_ref, vrow_ref, *,
                 r0, slot, tri, a_mask, forward, exact):
        self.q_ref, self.k_ref, self.v_ref, self.g_ref = q_ref, k_ref, v_ref, g_ref
        self.o_ref, self.s_ref, self.b_ref, self.a_ref, self.u_ref = o_ref, s_ref, b_ref, a_ref, u_ref
        self.vrow_ref = vrow_ref
        self.r0, self.slot = r0, slot
        self.rows = slice(r0, r0 + GLA_SUB)
        self.tri, self.a_mask, self.forward, self.exact = tri, a_mask, forward, exact
        nchunk = GLA_SUB // GLA_CHUNK
        self.last = GLA_CHUNK - 1 if forward else 0
        self.order = tuple(range(nchunk)) if forward else tuple(reversed(range(nchunk)))

    def cumulative_decay(self):
        g = self.g_ref[0, self.rows, :]
        self.b_ref[self.slot] = sum(_dot(self.tri, part) for part in _split3(g))

    def scores(self, head_masks, head_expand):
        b = self.b_ref[self.slot]
        q = self.q_ref[0, self.rows, :]
        self.qt = (q * jnp.exp(b)).astype(BF16)
        if not self.exact:
            kt = (self.k_ref[0, self.rows, :] * jnp.exp(-b)).astype(BF16)
        self.qs = {}
        for c in range(GLA_SUB // GLA_CHUNK):
            rows = slice(c * GLA_CHUNK, (c + 1) * GLA_CHUNK)
            qc = self.qt[rows]
            qs = jnp.concatenate([jnp.where(hm, qc, jnp.zeros_like(qc)) for hm in head_masks], axis=0)
            self.qs[c] = qs
            if self.exact:
                self.a_ref[self.slot, rows, :] = self._exact_intra(c, q[rows], b[rows], head_expand)
            else:
                a = _dot_nt(qs, kt[rows])
                self.a_ref[self.slot, c] = jnp.where(self.a_mask, a, 0.0).astype(BF16)

    def _exact_intra(self, c, qc, bc, head_expand):
        base = self.r0 + c * GLA_CHUNK
        self.vrow_ref[...] = self.v_ref[0, base:base + GLA_CHUNK, :].astype(F32)
        t = lax.broadcasted_iota(jnp.int32, (GLA_CHUNK, 1), 0)

        def body(s, acc):
            krow = self.k_ref[0, pl.ds(base + s, 1), :]
            brow = self.b_ref[self.slot, pl.ds(c * GLA_CHUNK + s, 1), :]
            valid = (t >= s) if self.forward else (t < s)
            decay = jnp.exp(jnp.where(valid, bc - brow, 0.0))
            x = jnp.where(valid, qc * krow * decay, 0.0).astype(BF16)
            w = _dot(x, head_expand)
            return acc + w * self.vrow_ref[pl.ds(s, 1), :]

        return lax.fori_loop(0, GLA_CHUNK, body, jnp.zeros((GLA_CHUNK, GLA_V_WIDTH), F32))

    def increments(self, chunk_masks):
        nchunk = GLA_SUB // GLA_CHUNK
        b = self.b_ref[self.slot]
        b_last = jnp.concatenate(
            [jnp.broadcast_to(b[c * GLA_CHUNK + self.last:c * GLA_CHUNK + self.last + 1, :],
                              (GLA_CHUNK, GLA_QK_WIDTH)) for c in range(nchunk)], axis=0)
        ku_t = (self.k_ref[0, self.rows, :] * jnp.exp(b_last - b)).T.astype(BF16)
        self.decay_t = jnp.exp(b_last.T)
        for c in range(nchunk):
            pair = slice((c // 2) * 2 * GLA_CHUNK, (c // 2 + 1) * 2 * GLA_CHUNK)
            vrows = slice(self.r0 + pair.start, self.r0 + pair.stop)
            ku_pair = ku_t[:, pair]
            ku_c = jnp.where(chunk_masks[c % 2], ku_pair, jnp.zeros_like(ku_pair))
            for h in range(GLA_HEADS):
                hrows = slice(h * GLA_DK, (h + 1) * GLA_DK)
                vcols = slice(h * GLA_DV, (h + 1) * GLA_DV)
                self.u_ref[self.slot, c, hrows, :] = _dot(ku_c[hrows], self.v_ref[0, vrows, vcols])

    def advance_state(self):
        s = self.s_ref[...]
        self.states = {}
        for c in self.order:
            self.states[c] = s.astype(BF16)
            col = c * GLA_CHUNK
            s = self.decay_t[:, col:col + 1] * s + self.u_ref[self.slot, c]
        self.s_ref[...] = s

    def output_chunk(self, c):
        sub_rows = slice(c * GLA_CHUNK, (c + 1) * GLA_CHUNK)
        rows = slice(self.r0 + sub_rows.start, self.r0 + sub_rows.stop)
        inter = _dot(self.qs[c], self.states[c])
        for h in range(GLA_HEADS):
            arows = slice(h * GLA_CHUNK, (h + 1) * GLA_CHUNK)
            vcols = slice(h * GLA_DV, (h + 1) * GLA_DV)
            if self.exact:
                intra = self.a_ref[self.slot, sub_rows, vcols]
            else:
                intra = _dot(self.a_ref[self.slot, c, arows, :], self.v_ref[0, rows, vcols])
            self.o_ref[0, rows, vcols] = inter[arows] + intra


def _gla_body(exact, qf_ref, kf_ref, vf_ref, gf_ref, qb_ref, kb_ref, vb_ref, gb_ref,
              of_ref, ob_ref, sf_ref, sb_ref, bf_ref, bb_ref, af_ref, ab_ref, uf_ref, ub_ref,
              vrow_ref):
    @pl.when(pl.program_id(1) == 0)
    def _():
        sf_ref[...] = jnp.zeros_like(sf_ref)
        sb_ref[...] = jnp.zeros_like(sb_ref)

    shift = GLA_CHUNK.bit_length() - 1
    row = lax.broadcasted_iota(jnp.int32, (GLA_SUB, GLA_SUB), 0)
    col = lax.broadcasted_iota(jnp.int32, (GLA_SUB, GLA_SUB), 1)
    same_chunk = (row >> shift) == (col >> shift)
    lower = jnp.where(same_chunk & (row >= col), 1.0, 0.0).astype(BF16)
    upper = jnp.where(same_chunk & (row <= col), 1.0, 0.0).astype(BF16)
    nsub = GLA_BLOCK // GLA_SUB
    lane = lax.broadcasted_iota(jnp.int32, (GLA_CHUNK, GLA_QK_WIDTH), 1)
    head_masks = [(lane >> shift) == h for h in range(GLA_HEADS)]
    pair_lane = lax.broadcasted_iota(jnp.int32, (GLA_QK_WIDTH, 2 * GLA_CHUNK), 1)
    chunk_masks = [(pair_lane >> shift) == j for j in range(2)]
    t = lax.broadcasted_iota(jnp.int32, (GLA_HEADS * GLA_CHUNK, GLA_CHUNK), 0) & (GLA_CHUNK - 1)
    s = lax.broadcasted_iota(jnp.int32, (GLA_HEADS * GLA_CHUNK, GLA_CHUNK), 1)
    head_expand = None
    if exact:
        erow = lax.broadcasted_iota(jnp.int32, (GLA_QK_WIDTH, GLA_V_WIDTH), 0) // GLA_DK
        ecol = lax.broadcasted_iota(jnp.int32, (GLA_QK_WIDTH, GLA_V_WIDTH), 1) // GLA_DV
        head_expand = jnp.where(erow == ecol, 1.0, 0.0).astype(BF16)
    fwd = [_GlaSubBlock(qf_ref, kf_ref, vf_ref, gf_ref, of_ref, sf_ref, bf_ref, af_ref, uf_ref,
                        vrow_ref, r0=i * GLA_SUB, slot=i, tri=lower, a_mask=t >= s, forward=True,
                        exact=exact) for i in range(nsub)]
    bwd = [_GlaSubBlock(qb_ref, kb_ref, vb_ref, gb_ref, ob_ref, sb_ref, bb_ref, ab_ref, ub_ref,
                        vrow_ref, r0=i * GLA_SUB, slot=i, tri=upper, a_mask=t < s, forward=False,
                        exact=exact) for i in reversed(range(nsub))]
    both = [blk for pair in zip(fwd, bwd) for blk in pair]
    for blk in both:
        blk.cumulative_decay()
    for blk in both:
        blk.scores(head_masks, head_expand)
    for blk in both:
        blk.increments(chunk_masks)
    for blk in both:
        blk.advance_state()
    for f, b in zip(fwd, bwd):
        for cf, cb in zip(f.order, b.order):
            f.output_chunk(cf)
            b.output_chunk(cb)


def _gla(pf, pb, *, exact):
    b, s, _ = pf.shape
    nb = s // GLA_BLOCK
    q, k, lg, v = pf, pf, pf, pb
    qj, kj, gj = PF_Q // GLA_QK_WIDTH, PF_K // GLA_QK_WIDTH, PF_GATES // GLA_QK_WIDTH
    vj = PB_GLA_V // GLA_V_WIDTH

    def fwd(width, j=0):
        return pl.BlockSpec((1, GLA_BLOCK, width), lambda bi, n: (bi, n, j))

    def bwd(width, j=0):
        return pl.BlockSpec((1, GLA_BLOCK, width), lambda bi, n: (bi, nb - 1 - n, j))

    nsub = GLA_BLOCK // GLA_SUB
    nchunk = GLA_SUB // GLA_CHUNK
    state = pltpu.VMEM((GLA_QK_WIDTH, GLA_DV), F32)
    decay = pltpu.VMEM((nsub, GLA_SUB, GLA_QK_WIDTH), F32)
    if exact:
        intra = pltpu.VMEM((nsub, GLA_SUB, GLA_V_WIDTH), F32)
    else:
        intra = pltpu.VMEM((nsub, nchunk, GLA_HEADS * GLA_CHUNK, GLA_CHUNK), BF16)
    increments = pltpu.VMEM((nsub, nchunk, GLA_QK_WIDTH, GLA_DV), F32)
    vrows = pltpu.VMEM((GLA_CHUNK, GLA_V_WIDTH), F32)
    return pl.pallas_call(
        functools.partial(_gla_body, exact),
        grid=(b, nb),
        in_specs=[fwd(GLA_QK_WIDTH, qj), fwd(GLA_QK_WIDTH, kj), fwd(GLA_V_WIDTH, vj),
                  fwd(GLA_QK_WIDTH, gj),
                  bwd(GLA_QK_WIDTH, qj), bwd(GLA_QK_WIDTH, kj), bwd(GLA_V_WIDTH, vj),
                  bwd(GLA_QK_WIDTH, gj + 1)],
        out_specs=[fwd(GLA_V_WIDTH), bwd(GLA_V_WIDTH)],
        out_shape=[jax.ShapeDtypeStruct((b, s, GLA_V_WIDTH), F32)] * 2,
        scratch_shapes=[state, state, decay, decay, intra, intra, increments, increments, vrows],
        compiler_params=_params("parallel", "arbitrary"),
        name="gla_exact" if exact else "gla",
    )(q, k, v, lg, q, k, v, lg)


def _mix_update(x, rows, na_ref, nag_ref, of_ref, ob_ref, gg_ref, gr_ref, wo_ref):
    na = _rmsnorm(na_ref[rows, :], nag_ref[...]).astype(BF16)
    y = _dot(na, wo_ref[:NA_WIDTH, :])
    o = of_ref[rows, :] + ob_ref[rows, :]
    r = gr_ref[rows, :]
    gate = r * jax.nn.sigmoid(r)
    heads = []
    for h in range(GLA_HEADS):
        cols = slice(h * GLA_DV, (h + 1) * GLA_DV)
        heads.append((_rmsnorm(o[:, cols], gg_ref[...]) * gate[:, cols]).astype(BF16))
    return x + y + _dot(jnp.concatenate(heads, axis=-1), wo_ref[NA_WIDTH:, :])


def _mix_ffn_body(final_norm, x_ref, na_ref, nag_ref, of_ref, ob_ref, gg_ref, gr_ref, woc_ref,
                  g_ref, wgc_ref, wuc_ref, wdc_ref, fg_ref, o_ref, wo_ref, wg_ref, wu_ref, wd_ref):
    step = pl.program_id(0)
    nstage = wg_ref.shape[1] // CAST_CHUNK
    _stage_weight_chunks(step, ((woc_ref, wo_ref, 1, wo_ref.shape[1] // CAST_CHUNK),
                                (wgc_ref, wg_ref, 1, nstage), (wuc_ref, wu_ref, 1, nstage),
                                (wdc_ref, wd_ref, 0, nstage)))

    @pl.when(step >= nstage)
    def _():
        subtiles = [slice(r, r + TOKEN_SUBTILE) for r in range(0, TM_TOKENS, TOKEN_SUBTILE)]
        for rows in subtiles:
            o_ref[rows, :] = _mix_update(x_ref[rows, :], rows, na_ref, nag_ref, of_ref, ob_ref,
                                         gg_ref, gr_ref, wo_ref)
        for rows in subtiles:
            y = _swiglu_update(o_ref[rows, :], g_ref, wg_ref, wu_ref, wd_ref)
            o_ref[rows, :] = _rmsnorm(y, fg_ref[...]) if final_norm else y


def _mix_ffn(x, na_o, na_gain, o_f, o_b, gla_gain, g_r, w_out, norm_g, wg, wu, wd, layer,
             final_g, final_norm):
    n, d = x.shape
    d_ff = wg.shape[2]
    tm = TM_TOKENS
    nstage = d_ff // CAST_CHUNK

    def tile(width):
        return _token_tile(tm, width, nstage)

    return pl.pallas_call(
        functools.partial(_mix_ffn_body, final_norm),
        grid=(nstage + n // tm,),
        in_specs=[tile(d), tile(NA_WIDTH), _const_spec((1, NA_WIDTH)), tile(GLA_V_WIDTH),
                  tile(GLA_V_WIDTH), _const_spec((1, GLA_DV)),
                  _token_tile(tm, GLA_V_WIDTH, nstage, PF_R // GLA_V_WIDTH),
                  _col_chunk_spec(layer, w_out.shape[1], d // CAST_CHUNK),
                  _const_spec((1, d)), _col_chunk_spec(layer, d, nstage),
                  _col_chunk_spec(layer, d, nstage), _row_chunk_spec(layer, d, nstage),
                  _const_spec((1, d))],
        out_specs=tile(d),
        out_shape=jax.ShapeDtypeStruct((n, d), F32),
        scratch_shapes=[pltpu.VMEM(w_out.shape[1:], BF16), pltpu.VMEM((d, d_ff), BF16),
                        pltpu.VMEM((d, d_ff), BF16), pltpu.VMEM((d_ff, d), BF16)],
        compiler_params=_params("arbitrary"),
        name="mix_ffn_final" if final_norm else "mix_ffn",
    )(x, na_o, na_gain, o_f, o_b, gla_gain, g_r, w_out, norm_g, wg, wu, wd, final_g)


def _gate_weights(w_f, b_f, w_b, b_b):
    w = jnp.zeros((CODE_PAD, 2 * GLA_QK_WIDTH), F32)
    w = w.at[:GLA_GATE_RANK, :GLA_QK_WIDTH].set(w_f)
    w = w.at[GLA_GATE_RANK:2 * GLA_GATE_RANK, GLA_QK_WIDTH:].set(w_b)
    return w, jnp.concatenate([b_f, b_b])[None, :]


def kernel(x, ffn1_norm, ffn1_wg, ffn1_wu, ffn1_wd, mix_norm, w_in, na_rpb, na_gain, w_gate_f, b_gate_f, w_gate_b, b_gate_b, gla_gain, w_out, ffn2_norm, ffn2_wg, ffn2_wu, ffn2_wd, final_norm):
    bsz, seq, d = x.shape
    depth = ffn1_norm.shape[0]
    n = bsz * seq
    xt = x.reshape(n, d)
    bias = _na_bias_tables(na_rpb)
    w_in_t = jnp.swapaxes(w_in, 1, 2)
    for l in range(depth):
        w_code = jnp.pad(w_in_t[l, MAIN_WIDTH:, :], ((0, CODE_PAD - 2 * GLA_GATE_RANK), (0, 0)))
        w_gate, b_gate = _gate_weights(w_gate_f[l], b_gate_f[l], w_gate_b[l], b_gate_b[l])
        xt, packed_bf16, packed_f32, min_decay = _ffn_inproj(
            xt, ffn1_norm[l][None], ffn1_wg, ffn1_wu, ffn1_wd, mix_norm[l][None], w_in_t, w_code,
            w_gate, b_gate, l)
        to3 = lambda a: a.reshape(bsz, seq, a.shape[-1])
        na_o = _na(to3(packed_bf16), bias[l])
        o_f, o_b = lax.cond(jnp.min(min_decay) >= -GLA_SAFE_DECAY,
                            functools.partial(_gla, exact=False), functools.partial(_gla, exact=True),
                            to3(packed_f32), to3(packed_bf16))
        xt = _mix_ffn(xt, na_o.reshape(n, NA_WIDTH), na_gain[l][None], o_f.reshape(n, GLA_V_WIDTH),
                      o_b.reshape(n, GLA_V_WIDTH), gla_gain[l][None], packed_f32, w_out,
                      ffn2_norm[l][None], ffn2_wg, ffn2_wu, ffn2_wd, l, final_norm[None],
                      final_norm=l == depth - 1)
    return xt.reshape(bsz, seq, d)
```
